```python
import jax, jax.numpy as jnp
from jax import lax
import numpy as np

D_MODEL = 2048
BATCH = 8
SEQ = 2048
DEPTH = 1

N_META = 16
POOL_WIDTH = D_MODEL // 2
POOL_WINDOWS = (2, 4, 8, 16)
N_POOL_GROUPS = len(POOL_WINDOWS)
POOL_GROUP_DIM = POOL_WIDTH // N_POOL_GROUPS
CONV_WIDTH = D_MODEL // 2
CONV_KERNEL = 31
D_FF = 4 * D_MODEL
IN_COLS = POOL_WIDTH + 2 * CONV_WIDTH + 2 * D_MODEL
RMS_EPS = 1e-6
LN_EPS = 1e-5

kernel_name = "hybrid_pool_conformer_gated_block"


def rms_norm(x, g):
    xf = x.astype(jnp.float32)
    y = xf * lax.rsqrt(jnp.mean(xf * xf, axis=-1, keepdims=True) + RMS_EPS)
    return (y * g.astype(jnp.float32)).astype(x.dtype)


def layer_norm(x, g, b):
    xf = x.astype(jnp.float32)
    mu = jnp.mean(xf, axis=-1, keepdims=True)
    var = jnp.mean(jnp.square(xf - mu), axis=-1, keepdims=True)
    y = (xf - mu) * lax.rsqrt(var + LN_EPS)
    return (y * g.astype(jnp.float32) + b.astype(jnp.float32)).astype(x.dtype)


def causal_multiscale_pool(z, w_grp, scale):
    B, L, _ = z.shape
    zf = z.astype(jnp.float32).reshape(B, L, N_POOL_GROUPS, POOL_GROUP_DIM)
    cs = jnp.cumsum(zf, axis=1)
    pos = jnp.arange(L)
    means = []
    for g, w in enumerate(POOL_WINDOWS):
        csg = cs[:, :, g]
        lag = jnp.pad(csg[:, : L - w], ((0, 0), (w, 0), (0, 0)))
        cnt = jnp.minimum(pos + 1, w).astype(jnp.float32)[None, :, None]
        means.append((csg - lag) / cnt)
    pooled = jnp.stack(means, axis=2)
    d = (pooled - zf).astype(z.dtype)
    y = jnp.einsum('blgc,gcd->blgd', d, w_grp).reshape(B, L, POOL_WIDTH)
    return y * scale


def conformer_conv(v, gate, w_dw, b_dw, ln_g, ln_b):
    a = v * jax.nn.sigmoid(gate)
    c = lax.conv_general_dilated(
        a, w_dw[:, None, :], window_strides=(1,),
        padding=[(CONV_KERNEL - 1, 0)],
        dimension_numbers=('NWC', 'WIO', 'NWC'),
        feature_group_count=CONV_WIDTH) + b_dw
    return jax.nn.silu(layer_norm(c, ln_g, ln_b))


def setup_inputs(seed: int = 0) -> dict:
    key = jax.random.key(seed)
    ks = jax.random.split(key, 24)
    f32 = jnp.float32
    n = lambda k, shape, s: jax.random.normal(k, shape, f32) * s
    gain = lambda k, shape: 1.0 + 0.05 * jax.random.normal(k, shape, f32)
    return {
        "x": jax.random.normal(ks[0], (BATCH, SEQ, D_MODEL), f32),
        "meta": n(ks[1], (N_META, D_MODEL), 1.0),
        "g_pre_mix": gain(ks[2], (DEPTH, D_MODEL)),
        "w_in": n(ks[3], (DEPTH, D_MODEL, IN_COLS), D_MODEL ** -0.5),
        "w_pool_grp": n(ks[4], (DEPTH, N_POOL_GROUPS, POOL_GROUP_DIM, POOL_GROUP_DIM), POOL_GROUP_DIM ** -0.5),
        "pool_scale": gain(ks[5], (DEPTH, POOL_WIDTH)),
        "w_pool_out": n(ks[6], (DEPTH, POOL_WIDTH, D_MODEL), POOL_WIDTH ** -0.5),
        "w_dw": n(ks[7], (DEPTH, CONV_KERNEL, CONV_WIDTH), CONV_KERNEL ** -0.5),
        "b_dw": n(ks[8], (DEPTH, CONV_WIDTH), 0.02),
        "conv_ln_g": gain(ks[9], (DEPTH, CONV_WIDTH)),
        "conv_ln_b": n(ks[10], (DEPTH, CONV_WIDTH), 0.02),
        "w_conv_out": n(ks[11], (DEPTH, CONV_WIDTH, D_MODEL), CONV_WIDTH ** -0.5),
        "w_o": n(ks[12], (DEPTH, D_MODEL, D_MODEL), D_MODEL ** -0.5),
        "g_post_mix": gain(ks[13], (DEPTH, D_MODEL)),
        "g_pre_mlp": gain(ks[14], (DEPTH, D_MODEL)),
        "w_up": n(ks[15], (DEPTH, D_MODEL, D_FF), D_MODEL ** -0.5),
        "w_down": n(ks[16], (DEPTH, D_FF, D_MODEL), D_FF ** -0.5),
        "g_post_mlp": gain(ks[17], (DEPTH, D_MODEL)),
    }


def reference(x, meta, g_pre_mix, w_in, w_pool_grp, pool_scale, w_pool_out,
              w_dw, b_dw, conv_ln_g, conv_ln_b, w_conv_out, w_o, g_post_mix,
              g_pre_mlp, w_up, w_down, g_post_mlp):
    B = x.shape[0]
    meta_b = jnp.broadcast_to(meta[None].astype(x.dtype), (B, N_META, D_MODEL))
    h = jnp.concatenate([meta_b, x], axis=1)
    splits = np.cumsum([POOL_WIDTH, CONV_WIDTH, CONV_WIDTH, D_MODEL]).tolist()
    for l in range(DEPTH):
        u = rms_norm(h, g_pre_mix[l])
        proj = u @ w_in[l]
        z_pool, v_conv, g_conv, gate_a, gate_b = jnp.split(proj, splits, axis=-1)
        y_a = causal_multiscale_pool(z_pool, w_pool_grp[l], pool_scale[l]) @ w_pool_out[l]
        y_b = conformer_conv(v_conv, g_conv, w_dw[l], b_dw[l],
                             conv_ln_g[l], conv_ln_b[l]) @ w_conv_out[l]
        m = jax.nn.sigmoid(gate_a) * y_a + jax.nn.sigmoid(gate_b) * y_b
        h = h + rms_norm(m @ w_o[l], g_post_mix[l])
        u = rms_norm(h, g_pre_mlp[l])
        f = jnp.square(jax.nn.relu(u @ w_up[l])) @ w_down[l]
        h = h + rms_norm(f, g_post_mlp[l])
    return h[:, N_META:]
```

```python
import functools

import jax
import jax.numpy as jnp
from jax import lax
from jax.experimental import pallas as pl
from jax.experimental.pallas import tpu as pltpu

D_MODEL = 2048
N_META = 16
POOL_WIDTH = D_MODEL // 2
POOL_WINDOWS = (2, 4, 8, 16)
POOL_GROUP_DIM = POOL_WIDTH // len(POOL_WINDOWS)
CONV_WIDTH = D_MODEL // 2
CONV_KERNEL = 31
D_FF = 4 * D_MODEL
IN_COLS = POOL_WIDTH + 2 * CONV_WIDTH + 2 * D_MODEL
RMS_EPS = 1e-6
LN_EPS = 1e-5

V7X_VMEM_LIMIT_BYTES = 56 * 1024 * 1024

F32 = jnp.float32
BF16 = jnp.bfloat16

IN_TN = 1024
IN_TM = 512
STEP_Z, STEP_V, STEP_G, STEP_GA, STEP_GB = 0, 1, 2, 3, 5
N_IN_STEPS_GLU = 3
N_IN_STEPS_ALL = IN_COLS // IN_TN

MIX_TM = 256
POOL_HALO = 16
CONV_HALO = 32
CONV_ROWS = 32
CONV_COLS = 256

MLP_TM = 512
MLP_TF = 1024


def _rms_norm(x, g):
    ms = jnp.mean(x * x, axis=-1, keepdims=True)
    return x * lax.rsqrt(ms + RMS_EPS) * g


def _in_proj_kernel(x_ref, g_ref, w_ref, z_ref, a_ref, *rest, with_gates):
    if with_gates:
        sga_ref, sgb_ref, u_scr, v_scr = rest
    else:
        u_scr, v_scr = rest
    j = pl.program_id(1)

    @pl.when(j == 0)
    def _():
        u_scr[...] = _rms_norm(x_ref[...], g_ref[...]).astype(BF16)

    p = jnp.dot(u_scr[...], w_ref[...], preferred_element_type=F32)

    @pl.when(j == STEP_Z)
    def _():
        z_ref[...] = p

    @pl.when(j == STEP_V)
    def _():
        v_scr[...] = p

    @pl.when(j == STEP_G)
    def _():
        a_ref[...] = v_scr[...] * jax.nn.sigmoid(p)

    if with_gates:
        @pl.when(jnp.logical_and(j >= STEP_GA, j < STEP_GB))
        def _():
            sga_ref[...] = jax.nn.sigmoid(p).astype(BF16)

        @pl.when(j >= STEP_GB)
        def _():
            sgb_ref[...] = jax.nn.sigmoid(p).astype(BF16)


def _in_proj(x2d, g_pre, w_in, *, tm, with_gates):
    m = x2d.shape[0]
    n_steps = N_IN_STEPS_ALL if with_gates else N_IN_STEPS_GLU
    row = lambda i, j: (i, 0)
    out_shape = [jax.ShapeDtypeStruct((m, POOL_WIDTH), F32),
                 jax.ShapeDtypeStruct((m, CONV_WIDTH), F32)]
    out_specs = [pl.BlockSpec((tm, IN_TN), row), pl.BlockSpec((tm, IN_TN), row)]
    if with_gates:
        out_shape += [jax.ShapeDtypeStruct((m, D_MODEL), BF16)] * 2
        out_specs += [
            pl.BlockSpec((tm, IN_TN), lambda i, j: (i, jnp.clip(j - STEP_GA, 0, 1))),
            pl.BlockSpec((tm, IN_TN), lambda i, j: (i, jnp.clip(j - STEP_GB, 0, 1))),
        ]
    return pl.pallas_call(
        functools.partial(_in_proj_kernel, with_gates=with_gates),
        out_shape=out_shape,
        grid=(m // tm, n_steps),
        in_specs=[
            pl.BlockSpec((tm, D_MODEL), row),
            pl.BlockSpec((1, D_MODEL), lambda i, j: (0, 0)),
            pl.BlockSpec((D_MODEL, IN_TN), lambda i, j: (0, j)),
        ],
        out_specs=out_specs,
        scratch_shapes=[pltpu.VMEM((tm, D_MODEL), BF16),
                        pltpu.VMEM((tm, IN_TN), F32)],
        compiler_params=pltpu.CompilerParams(
            dimension_semantics=("arbitrary", "arbitrary"),
            vmem_limit_bytes=V7X_VMEM_LIMIT_BYTES),
        name="in_proj_gates" if with_gates else "in_proj_meta",
    )(x2d, g_pre, w_in)


def _mixer_kernel(z_ref, a_ref, sga_ref, sgb_ref, x_ref, zm_ref, am_ref,
                  wgrp_ref, pscale_ref, wpo_ref, wdw_ref, bdw_ref,
                  lng_ref, lnb_ref, wco_ref, wo_ref, gpm_ref,
                  o_ref, zs, cs, conv_scr):
    tm = z_ref.shape[0]
    t = pl.program_id(1)

    @pl.when(t == 0)
    def _():
        zs[0:POOL_HALO, :] = zm_ref[...]
        cs[0:CONV_HALO - N_META, :] = jnp.zeros((CONV_HALO - N_META, CONV_WIDTH), F32)
        cs[CONV_HALO - N_META:CONV_HALO, :] = am_ref[...]

    @pl.when(t > 0)
    def _():
        zs[0:POOL_HALO, :] = zs[tm:tm + POOL_HALO, :]
        cs[0:CONV_HALO, :] = cs[tm:tm + CONV_HALO, :]

    zs[POOL_HALO:POOL_HALO + tm, :] = z_ref[...]
    cs[CONV_HALO:CONV_HALO + tm, :] = a_ref[...]

    ya_parts = []
    for g, w in enumerate(POOL_WINDOWS):
        c0 = g * POOL_GROUP_DIM
        c1 = c0 + POOL_GROUP_DIM
        zt = zs[POOL_HALO:POOL_HALO + tm, c0:c1]
        s = zt
        for k in range(1, w):
            s = s + zs[POOL_HALO - k:POOL_HALO - k + tm, c0:c1]
        d = (s * (1.0 / w) - zt).astype(BF16)
        y = jnp.dot(d, wgrp_ref[g], preferred_element_type=F32)
        ya_parts.append((y * pscale_ref[:, c0:c1]).astype(BF16))
    pooled = jnp.concatenate(ya_parts, axis=-1)
    y_a = jnp.dot(pooled, wpo_ref[...], preferred_element_type=F32)

    for cb in range(CONV_WIDTH // CONV_COLS):
        c0 = cb * CONV_COLS
        c1 = c0 + CONV_COLS
        for r0 in range(0, tm, CONV_ROWS):
            acc = jnp.broadcast_to(bdw_ref[:, c0:c1], (CONV_ROWS, CONV_COLS))
            for k in range(CONV_KERNEL):
                off = r0 + CONV_HALO - (CONV_KERNEL - 1) + k
                acc = acc + cs[off:off + CONV_ROWS, c0:c1] * wdw_ref[k:k + 1, c0:c1]
            conv_scr[r0:r0 + CONV_ROWS, c0:c1] = acc

    c = conv_scr[...]
    mu = jnp.mean(c, axis=-1, keepdims=True)
    cc = c - mu
    var = jnp.mean(cc * cc, axis=-1, keepdims=True)
    ln = cc * lax.rsqrt(var + LN_EPS) * lng_ref[...] + lnb_ref[...]
    act = (ln * jax.nn.sigmoid(ln)).astype(BF16)
    y_b = jnp.dot(act, wco_ref[...], preferred_element_type=F32)

    m = sga_ref[...].astype(F32) * y_a + sgb_ref[...].astype(F32) * y_b
    mo = jnp.dot(m.astype(BF16), wo_ref[...], preferred_element_type=F32)
    o_ref[...] = x_ref[...] + _rms_norm(mo, gpm_ref[...])


def _mixer(z, a, sga, sgb, x2d, z_meta, a_meta, w_grp, pool_scale, w_pool_out,
           w_dw, b_dw, ln_g, ln_b, w_conv_out, w_o, g_post, *, batch, seq):
    tm = MIX_TM
    nt = seq // tm
    row = lambda b, t: (b * nt + t, 0)
    const2 = lambda b, t: (0, 0)
    const3 = lambda b, t: (0, 0, 0)
    once = pl.Buffered(1)
    return pl.pallas_call(
        _mixer_kernel,
        out_shape=jax.ShapeDtypeStruct((batch * seq, D_MODEL), F32),
        grid=(batch, nt),
        in_specs=[
            pl.BlockSpec((tm, POOL_WIDTH), row),
            pl.BlockSpec((tm, CONV_WIDTH), row),
            pl.BlockSpec((tm, D_MODEL), row),
            pl.BlockSpec((tm, D_MODEL), row),
            pl.BlockSpec((tm, D_MODEL), row),
            pl.BlockSpec((N_META, POOL_WIDTH), const2, pipeline_mode=once),
            pl.BlockSpec((N_META, CONV_WIDTH), const2, pipeline_mode=once),
            pl.BlockSpec((len(POOL_WINDOWS), POOL_GROUP_DIM, POOL_GROUP_DIM), const3,
                         pipeline_mode=once),
            pl.BlockSpec((1, POOL_WIDTH), const2, pipeline_mode=once),
            pl.BlockSpec((POOL_WIDTH, D_MODEL), const2, pipeline_mode=once),
            pl.BlockSpec((CONV_KERNEL, CONV_WIDTH), const2, pipeline_mode=once),
            pl.BlockSpec((1, CONV_WIDTH), const2, pipeline_mode=once),
            pl.BlockSpec((1, CONV_WIDTH), const2, pipeline_mode=once),
            pl.BlockSpec((1, CONV_WIDTH), const2, pipeline_mode=once),
            pl.BlockSpec((CONV_WIDTH, D_MODEL), const2, pipeline_mode=once),
            pl.BlockSpec((D_MODEL, D_MODEL), const2, pipeline_mode=once),
            pl.BlockSpec((1, D_MODEL), const2, pipeline_mode=once),
        ],
        out_specs=pl.BlockSpec((tm, D_MODEL), row),
        scratch_shapes=[
            pltpu.VMEM((POOL_HALO + tm, POOL_WIDTH), F32),
            pltpu.VMEM((CONV_HALO + tm, CONV_WIDTH), F32),
            pltpu.VMEM((tm, CONV_WIDTH), F32),
        ],
        compiler_params=pltpu.CompilerParams(
            dimension_semantics=("arbitrary", "arbitrary"),
            vmem_limit_bytes=V7X_VMEM_LIMIT_BYTES),
        name="mixer",
    )(z, a, sga, sgb, x2d, z_meta, a_meta, w_grp, pool_scale, w_pool_out,
      w_dw, b_dw, ln_g, ln_b, w_conv_out, w_o, g_post)


def _mlp_kernel(h_ref, g1_ref, wup_ref, wdn_ref, g2_ref, o_ref, u_scr):
    k = pl.program_id(1)

    @pl.when(k == 0)
    def _():
        u_scr[...] = _rms_norm(h_ref[...], g1_ref[...]).astype(BF16)

    up = jnp.dot(u_scr[...], wup_ref[...], preferred_element_type=F32)
    act = jnp.square(jnp.maximum(up, 0.0)).astype(BF16)
    part = jnp.dot(act, wdn_ref[...], preferred_element_type=F32)

    @pl.when(k == 0)
    def _():
        o_ref[...] = part

    @pl.when(k > 0)
    def _():
        o_ref[...] += part

    @pl.when(k == pl.num_programs(1) - 1)
    def _():
        o_ref[...] = h_ref[...] + _rms_norm(o_ref[...], g2_ref[...])


def _mlp(h, g_pre, w_up, w_down, g_post):
    m = h.shape[0]
    row = lambda i, k: (i, 0)
    const2 = lambda i, k: (0, 0)
    return pl.pallas_call(
        _mlp_kernel,
        out_shape=jax.ShapeDtypeStruct((m, D_MODEL), F32),
        grid=(m // MLP_TM, D_FF // MLP_TF),
        in_specs=[
            pl.BlockSpec((MLP_TM, D_MODEL), row),
            pl.BlockSpec((1, D_MODEL), const2),
            pl.BlockSpec((D_MODEL, MLP_TF), lambda i, k: (0, k)),
            pl.BlockSpec((MLP_TF, D_MODEL), lambda i, k: (k, 0)),
            pl.BlockSpec((1, D_MODEL), const2),
        ],
        out_specs=pl.BlockSpec((MLP_TM, D_MODEL), row),
        scratch_shapes=[pltpu.VMEM((MLP_TM, D_MODEL), BF16)],
        compiler_params=pltpu.CompilerParams(
            dimension_semantics=("arbitrary", "arbitrary"),
            vmem_limit_bytes=V7X_VMEM_LIMIT_BYTES),
        name="mlp",
    )(h, g_pre, w_up, w_down, g_post)


def kernel(x, meta, g_pre_mix, w_in, w_pool_grp, pool_scale, w_pool_out, w_dw, b_dw, conv_ln_g, conv_ln_b, w_conv_out, w_o, g_post_mix, g_pre_mlp, w_up, w_down, g_post_mlp):
    batch, seq, _ = x.shape
    depth = w_in.shape[0]
    h = x.reshape(batch * seq, D_MODEL)
    meta_h = meta.astype(x.dtype)
    assert depth == 1, "meta rows are only carried as first-layer causal history"
    for l in range(depth):
        w_in_l = w_in[l].astype(BF16)
        g_pre = g_pre_mix[l][None]
        z_meta, a_meta = _in_proj(meta_h, g_pre, w_in_l, tm=N_META, with_gates=False)
        z, a, sga, sgb = _in_proj(h, g_pre, w_in_l, tm=IN_TM, with_gates=True)
        h = _mixer(z, a, sga, sgb, h, z_meta, a_meta,
                   w_pool_grp[l].astype(BF16), pool_scale[l][None],
                   w_pool_out[l].astype(BF16), w_dw[l], b_dw[l][None],
                   conv_ln_g[l][None], conv_ln_b[l][None],
                   w_conv_out[l].astype(BF16), w_o[l].astype(BF16),
                   g_post_mix[l][None], batch=batch, seq=seq)
        h = _mlp(h, g_pre_mlp[l][None], w_up[l].astype(BF16),
                 w_down[l].astype(BF16), g_post_mlp[l][None])
    return h.reshape(batch, seq, D_MODEL)
```

```python
import functools

import jax
import jax.numpy as jnp
from jax import lax
from jax.experimental import pallas as pl
from jax.experimental.pallas import tpu as pltpu

D_MODEL = 2048
N_META = 16
POOL_WIDTH = D_MODEL // 2
POOL_WINDOWS = (2, 4, 8, 16)
POOL_GROUP_DIM = POOL_WIDTH // len(POOL_WINDOWS)
CONV_WIDTH = D_MODEL // 2
CONV_KERNEL = 31
D_FF = 4 * D_MODEL
IN_COLS = POOL_WIDTH + 2 * CONV_WIDTH + 2 * D_MODEL
RMS_EPS = 1e-6
LN_EPS = 1e-5

V7X_VMEM_LIMIT_BYTES = 56 * 1024 * 1024

F32 = jnp.float32
BF16 = jnp.bfloat16

IN_TN = 1024
IN_TM = 512
STEP_Z, STEP_V, STEP_G, STEP_GA, STEP_GB = 0, 1, 2, 3, 5
N_IN_STEPS_GLU = 3
N_IN_STEPS_ALL = IN_COLS // IN_TN

MIX_TM = 256
POOL_HALO = 16
CONV_HALO = 32
CONV_ROWS = 32
CONV_COLS = 256
SUBLANES = 8

MLP_TM = 512
MLP_TF = 1024


def _rms_norm(x, g):
    ms = jnp.mean(x * x, axis=-1, keepdims=True)
    return x * lax.rsqrt(ms + RMS_EPS) * g


def _in_proj_kernel(x_ref, g_ref, w_ref, z_ref, a_ref, *rest, with_gates):
    if with_gates:
        sga_ref, sgb_ref, u_scr, v_scr = rest
    else:
        u_scr, v_scr = rest
    j = pl.program_id(1)

    @pl.when(j == 0)
    def _():
        u_scr[...] = _rms_norm(x_ref[...], g_ref[...]).astype(BF16)

    def proj():
        return jnp.dot(u_scr[...], w_ref[...], preferred_element_type=F32)

    @pl.when(j == STEP_Z)
    def _():
        z_ref[...] = proj()

    @pl.when(j == STEP_V)
    def _():
        v_scr[...] = proj()

    @pl.when(j == STEP_G)
    def _():
        a_ref[...] = v_scr[...] * jax.nn.sigmoid(proj())

    if with_gates:
        @pl.when(jnp.logical_and(j >= STEP_GA, j < STEP_GB))
        def _():
            sga_ref[...] = jax.nn.sigmoid(proj()).astype(BF16)

        @pl.when(j >= STEP_GB)
        def _():
            sgb_ref[...] = jax.nn.sigmoid(proj()).astype(BF16)


def _in_proj(x2d, g_pre, w_in, *, tm, with_gates):
    m = x2d.shape[0]
    n_steps = N_IN_STEPS_ALL if with_gates else N_IN_STEPS_GLU
    row = lambda i, j: (i, 0)
    out_shape = [jax.ShapeDtypeStruct((m, POOL_WIDTH), F32),
                 jax.ShapeDtypeStruct((m, CONV_WIDTH), F32)]
    out_specs = [pl.BlockSpec((tm, IN_TN), row), pl.BlockSpec((tm, IN_TN), row)]
    if with_gates:
        out_shape += [jax.ShapeDtypeStruct((m, D_MODEL), BF16)] * 2
        out_specs += [
            pl.BlockSpec((tm, IN_TN), lambda i, j: (i, jnp.clip(j - STEP_GA, 0, 1))),
            pl.BlockSpec((tm, IN_TN), lambda i, j: (i, jnp.clip(j - STEP_GB, 0, 1))),
        ]
    return pl.pallas_call(
        functools.partial(_in_proj_kernel, with_gates=with_gates),
        out_shape=out_shape,
        grid=(m // tm, n_steps),
        in_specs=[
            pl.BlockSpec((tm, D_MODEL), row),
            pl.BlockSpec((1, D_MODEL), lambda i, j: (0, 0)),
            pl.BlockSpec((D_MODEL, IN_TN), lambda i, j: (0, j)),
        ],
        out_specs=out_specs,
        scratch_shapes=[pltpu.VMEM((tm, D_MODEL), BF16),
                        pltpu.VMEM((tm, IN_TN), F32)],
        compiler_params=pltpu.CompilerParams(
            dimension_semantics=("arbitrary", "arbitrary"),
            vmem_limit_bytes=V7X_VMEM_LIMIT_BYTES),
        name="in_proj_gates" if with_gates else "in_proj_meta",
    )(x2d, g_pre, w_in)


def _mixer_kernel(z_ref, a_ref, sga_ref, sgb_ref, x_ref, zm_ref, am_ref,
                  wgrp_ref, pscale_ref, wpo_ref, wdw_ref, bdw_ref,
                  lng_ref, lnb_ref, wco_ref, wo_ref, gpm_ref,
                  o_ref, zs, cs, conv_scr):
    tm = z_ref.shape[0]
    t = pl.program_id(1)

    @pl.when(t == 0)
    def _():
        zs[0:POOL_HALO, :] = zm_ref[...]
        cs[0, 0:CONV_HALO - N_META, :] = jnp.zeros((CONV_HALO - N_META, CONV_WIDTH), F32)
        cs[0, CONV_HALO - N_META:CONV_HALO, :] = am_ref[...]

    @pl.when(t > 0)
    def _():
        zs[0:POOL_HALO, :] = zs[tm:tm + POOL_HALO, :]
        cs[0, 0:CONV_HALO, :] = cs[0, tm:tm + CONV_HALO, :]

    zs[POOL_HALO:POOL_HALO + tm, :] = z_ref[...]
    cs[0, CONV_HALO:CONV_HALO + tm, :] = a_ref[...]

    n_rows = CONV_HALO + tm
    conv_in = cs[0]
    for s in range(1, SUBLANES):
        cs[s] = pltpu.roll(conv_in, n_rows - s, axis=0)

    ya_parts = []
    for g, w in enumerate(POOL_WINDOWS):
        c0 = g * POOL_GROUP_DIM
        c1 = c0 + POOL_GROUP_DIM
        zt = zs[POOL_HALO:POOL_HALO + tm, c0:c1]
        s = zt
        for k in range(1, w):
            s = s + zs[POOL_HALO - k:POOL_HALO - k + tm, c0:c1]
        d = (s * (1.0 / w) - zt).astype(BF16)
        y = jnp.dot(d, wgrp_ref[g], preferred_element_type=F32)
        ya_parts.append((y * pscale_ref[:, c0:c1]).astype(BF16))
    pooled = jnp.concatenate(ya_parts, axis=-1)
    y_a = jnp.dot(pooled, wpo_ref[...], preferred_element_type=F32)

    for cb in range(CONV_WIDTH // CONV_COLS):
        c0 = cb * CONV_COLS
        c1 = c0 + CONV_COLS
        for r0 in range(0, tm, CONV_ROWS):
            acc = jnp.broadcast_to(bdw_ref[:, c0:c1], (CONV_ROWS, CONV_COLS))
            for k in range(CONV_KERNEL):
                q, s = divmod(r0 + CONV_HALO - (CONV_KERNEL - 1) + k, SUBLANES)
                rows = slice(q * SUBLANES, q * SUBLANES + CONV_ROWS)
                wk = wdw_ref[k, :, c0:c1]
                acc = acc + cs[s, rows, c0:c1] * jnp.concatenate(
                    [wk] * (CONV_ROWS // SUBLANES), axis=0)
            conv_scr[r0:r0 + CONV_ROWS, c0:c1] = acc

    c = conv_scr[...]
    mu = jnp.mean(c, axis=-1, keepdims=True)
    cc = c - mu
    var = jnp.mean(cc * cc, axis=-1, keepdims=True)
    ln = cc * lax.rsqrt(var + LN_EPS) * lng_ref[...] + lnb_ref[...]
    act = (ln * jax.nn.sigmoid(ln)).astype(BF16)
    y_b = jnp.dot(act, wco_ref[...], preferred_element_type=F32)

    m = sga_ref[...].astype(F32) * y_a + sgb_ref[...].astype(F32) * y_b
    mo = jnp.dot(m.astype(BF16), wo_ref[...], preferred_element_type=F32)
    o_ref[...] = x_ref[...] + _rms_norm(mo, gpm_ref[...])


def _mixer(z, a, sga, sgb, x2d, z_meta, a_meta, w_grp, pool_scale, w_pool_out,
           w_dw, b_dw, ln_g, ln_b, w_conv_out, w_o, g_post, *, batch, seq):
    tm = MIX_TM
    nt = seq // tm
    row = lambda b, t: (b * nt + t, 0)
    const2 = lambda b, t: (0, 0)
    const3 = lambda b, t: (0, 0, 0)
    once = pl.Buffered(1)
    return pl.pallas_call(
        _mixer_kernel,
        out_shape=jax.ShapeDtypeStruct((batch * seq, D_MODEL), F32),
        grid=(batch, nt),
        in_specs=[
            pl.BlockSpec((tm, POOL_WIDTH), row),
            pl.BlockSpec((tm, CONV_WIDTH), row),
            pl.BlockSpec((tm, D_MODEL), row),
            pl.BlockSpec((tm, D_MODEL), row),
            pl.BlockSpec((tm, D_MODEL), row),
            pl.BlockSpec((N_META, POOL_WIDTH), const2, pipeline_mode=once),
            pl.BlockSpec((N_META, CONV_WIDTH), const2, pipeline_mode=once),
            pl.BlockSpec((len(POOL_WINDOWS), POOL_GROUP_DIM, POOL_GROUP_DIM), const3,
                         pipeline_mode=once),
            pl.BlockSpec((1, POOL_WIDTH), const2, pipeline_mode=once),
            pl.BlockSpec((POOL_WIDTH, D_MODEL), const2, pipeline_mode=once),
            pl.BlockSpec((CONV_KERNEL, SUBLANES, CONV_WIDTH), const3, pipeline_mode=once),
            pl.BlockSpec((1, CONV_WIDTH), const2, pipeline_mode=once),
            pl.BlockSpec((1, CONV_WIDTH), const2, pipeline_mode=once),
            pl.BlockSpec((1, CONV_WIDTH), const2, pipeline_mode=once),
            pl.BlockSpec((CONV_WIDTH, D_MODEL), const2, pipeline_mode=once),
            pl.BlockSpec((D_MODEL, D_MODEL), const2, pipeline_mode=once),
            pl.BlockSpec((1, D_MODEL), const2, pipeline_mode=once),
        ],
        out_specs=pl.BlockSpec((tm, D_MODEL), row),
        scratch_shapes=[
            pltpu.VMEM((POOL_HALO + tm, POOL_WIDTH), F32),
            pltpu.VMEM((SUBLANES, CONV_HALO + tm, CONV_WIDTH), F32),
            pltpu.VMEM((tm, CONV_WIDTH), F32),
        ],
        compiler_params=pltpu.CompilerParams(
            dimension_semantics=("arbitrary", "arbitrary"),
            vmem_limit_bytes=V7X_VMEM_LIMIT_BYTES),
        name="mixer",
    )(z, a, sga, sgb, x2d, z_meta, a_meta, w_grp, pool_scale, w_pool_out,
      w_dw, b_dw, ln_g, ln_b, w_conv_out, w_o, g_post)


def _mlp_kernel(h_ref, g1_ref, wup_ref, wdn_ref, g2_ref, o_ref, u_scr):
    k = pl.program_id(1)

    @pl.when(k == 0)
    def _():
        u_scr[...] = _rms_norm(h_ref[...], g1_ref[...]).astype(BF16)

        o_ref[...] = jnp.zeros(o_ref.shape, F32)

    up = jnp.dot(u_scr[...], wup_ref[...], preferred_element_type=F32)
    act = jnp.square(jnp.maximum(up, 0.0)).astype(BF16)
    o_ref[...] += jnp.dot(act, wdn_ref[...], preferred_element_type=F32)

    @pl.when(k == pl.num_programs(1) - 1)
    def _():
        o_ref[...] = h_ref[...] + _rms_norm(o_ref[...], g2_ref[...])


def _mlp(h, g_pre, w_up, w_down, g_post):
    m = h.shape[0]
    row = lambda i, k: (i, 0)
    const2 = lambda i, k: (0, 0)
    return pl.pallas_call(
        _mlp_kernel,
        out_shape=jax.ShapeDtypeStruct((m, D_MODEL), F32),
        grid=(m // MLP_TM, D_FF // MLP_TF),
        in_specs=[
            pl.BlockSpec((MLP_TM, D_MODEL), row),
            pl.BlockSpec((1, D_MODEL), const2),
            pl.BlockSpec((D_MODEL, MLP_TF), lambda i, k: (0, k)),
            pl.BlockSpec((MLP_TF, D_MODEL), lambda i, k: (k, 0)),
            pl.BlockSpec((1, D_MODEL), const2),
        ],
        out_specs=pl.BlockSpec((MLP_TM, D_MODEL), row),
        scratch_shapes=[pltpu.VMEM((MLP_TM, D_MODEL), BF16)],
        compiler_params=pltpu.CompilerParams(
            dimension_semantics=("arbitrary", "arbitrary"),
            vmem_limit_bytes=V7X_VMEM_LIMIT_BYTES),
        name="mlp",
    )(h, g_pre, w_up, w_down, g_post)


def kernel(x, meta, g_pre_mix, w_in, w_pool_grp, pool_scale, w_pool_out, w_dw, b_dw, conv_ln_g, conv_ln_b, w_conv_out, w_o, g_post_mix, g_pre_mlp, w_up, w_down, g_post_mlp):
    batch, seq, _ = x.shape
    depth = w_in.shape[0]
    h = x.reshape(batch * seq, D_MODEL)
    meta_h = meta.astype(x.dtype)
    assert depth == 1, "meta rows are only carried as first-layer causal history"
    for l in range(depth):
        w_in_l = w_in[l].astype(BF16)
        g_pre = g_pre_mix[l][None]
        z_meta, a_meta = _in_proj(meta_h, g_pre, w_in_l, tm=N_META, with_gates=False)
        z, a, sga, sgb = _in_proj(h, g_pre, w_in_l, tm=IN_TM, with_gates=True)
        h = _mixer(z, a, sga, sgb, h, z_meta, a_meta,
                   w_pool_grp[l].astype(BF16), pool_scale[l][None],
                   w_pool_out[l].astype(BF16),
                   jnp.broadcast_to(w_dw[l][:, None, :], (CONV_KERNEL, SUBLANES, CONV_WIDTH)),
                   b_dw[l][None],
                   conv_ln_g[l][None], conv_ln_b[l][None],
                   w_conv_out[l].astype(BF16), w_o[l].astype(BF16),
                   g_post_mix[l][None], batch=batch, seq=seq)
        h = _mlp(h, g_pre_mlp[l][None], w_up[l].astype(BF16),
                 w_down[l].astype(BF16), g_post_mlp[l][None])
    return h.reshape(batch, seq, D_MODEL)
```

```python
import functools

import jax
import jax.numpy as jnp
from jax import lax
from jax.experimental import pallas as pl
from jax.experimental.pallas import tpu as pltpu

D_MODEL = 2048
N_META = 16
POOL_WIDTH = D_MODEL // 2
POOL_WINDOWS = (2, 4, 8, 16)
POOL_GROUP_DIM = POOL_WIDTH // len(POOL_WINDOWS)
CONV_WIDTH = D_MODEL // 2
CONV_KERNEL = 31
D_FF = 4 * D_MODEL
IN_COLS = POOL_WIDTH + 2 * CONV_WIDTH + 2 * D_MODEL
RMS_EPS = 1e-6
LN_EPS = 1e-5

V7X_VMEM_LIMIT_BYTES = 56 * 1024 * 1024
SUBLANES = 8

F32 = jnp.float32
BF16 = jnp.bfloat16

IN_TN = 512
IN_TM = 1024
N_Z_STEPS = POOL_WIDTH // IN_TN
N_GLU_PAIRS = CONV_WIDTH // IN_TN
N_GATE_STEPS = D_MODEL // IN_TN
STEP_GLU = N_Z_STEPS
STEP_GA = STEP_GLU + 2 * N_GLU_PAIRS
STEP_GB = STEP_GA + N_GATE_STEPS
N_IN_STEPS_ALL = STEP_GB + N_GATE_STEPS

MIX_TM = 256
POOL_HALO = 16
CONV_HALO = 32
CONV_ROWS = 32
CONV_COLS = 256

MLP_TM = 512
MLP_TF = 1024
MLP_ROW_BLOCKS = 2


def _rms_norm(x, g):
    ms = jnp.mean(x * x, axis=-1, keepdims=True)
    return x * lax.rsqrt(ms + RMS_EPS) * g


def _glu_pair(j):
    r = j - STEP_GLU
    return r // 2, r % 2


def _w_in_block(j):
    pair, is_g = _glu_pair(j)
    glu_block = N_Z_STEPS + is_g * N_GLU_PAIRS + pair
    return jnp.where(jnp.logical_and(j >= STEP_GLU, j < STEP_GA), glu_block, j)


def _in_proj_kernel(x_ref, g_ref, w_ref, z_ref, a_ref, *rest, with_gates):
    if with_gates:
        sga_ref, sgb_ref, u_scr, v_scr = rest
    else:
        u_scr, v_scr = rest
    j = pl.program_id(1)
    is_glu = jnp.logical_and(j >= STEP_GLU, j < STEP_GA)
    is_g_half = _glu_pair(j)[1] == 1

    @pl.when(j == 0)
    def _():
        u_scr[...] = _rms_norm(x_ref[...], g_ref[...]).astype(BF16)

    def proj():
        return jnp.dot(u_scr[...], w_ref[...], preferred_element_type=F32)

    @pl.when(j < STEP_GLU)
    def _():
        z_ref[...] = proj()

    @pl.when(jnp.logical_and(is_glu, jnp.logical_not(is_g_half)))
    def _():
        v_scr[...] = proj()

    @pl.when(jnp.logical_and(is_glu, is_g_half))
    def _():
        a_ref[...] = v_scr[...] * jax.nn.sigmoid(proj())

    if with_gates:
        @pl.when(jnp.logical_and(j >= STEP_GA, j < STEP_GB))
        def _():
            sga_ref[...] = jax.nn.sigmoid(proj()).astype(BF16)

        @pl.when(j >= STEP_GB)
        def _():
            sgb_ref[...] = jax.nn.sigmoid(proj()).astype(BF16)


def _in_proj(x2d, g_pre, w_in, *, tm, with_gates):
    m = x2d.shape[0]
    n_steps = N_IN_STEPS_ALL if with_gates else STEP_GA
    row = lambda i, j: (i, 0)
    out_shape = [jax.ShapeDtypeStruct((m, POOL_WIDTH), F32),
                 jax.ShapeDtypeStruct((m, CONV_WIDTH), F32)]
    out_specs = [
        pl.BlockSpec((tm, IN_TN), lambda i, j: (i, jnp.clip(j, 0, N_Z_STEPS - 1))),
        pl.BlockSpec((tm, IN_TN), lambda i, j: (i, jnp.clip(_glu_pair(j)[0], 0, N_GLU_PAIRS - 1))),
    ]
    if with_gates:
        out_shape += [jax.ShapeDtypeStruct((m, D_MODEL), BF16)] * 2
        out_specs += [
            pl.BlockSpec((tm, IN_TN), lambda i, j: (i, jnp.clip(j - STEP_GA, 0, N_GATE_STEPS - 1))),
            pl.BlockSpec((tm, IN_TN), lambda i, j: (i, jnp.clip(j - STEP_GB, 0, N_GATE_STEPS - 1))),
        ]
    return pl.pallas_call(
        functools.partial(_in_proj_kernel, with_gates=with_gates),
        out_shape=out_shape,
        grid=(m // tm, n_steps),
        in_specs=[
            pl.BlockSpec((tm, D_MODEL), row),
            pl.BlockSpec((1, D_MODEL), lambda i, j: (0, 0)),
            pl.BlockSpec((D_MODEL, IN_TN), lambda i, j: (0, _w_in_block(j))),
        ],
        out_specs=out_specs,
        scratch_shapes=[pltpu.VMEM((tm, D_MODEL), BF16),
                        pltpu.VMEM((tm, IN_TN), F32)],
        compiler_params=pltpu.CompilerParams(
            dimension_semantics=("arbitrary", "arbitrary"),
            vmem_limit_bytes=V7X_VMEM_LIMIT_BYTES),
        name="in_proj_gates" if with_gates else "in_proj_meta",
    )(x2d, g_pre, w_in)


def _mixer_kernel(z_ref, a_ref, sga_ref, sgb_ref, x_ref, zm_ref, am_ref,
                  wgrp_ref, pscale_ref, wpo_ref, wdw_ref, bdw_ref,
                  lng_ref, lnb_ref, wco_ref, wo_ref, gpm_ref,
                  o_ref, zs, cs, conv_scr):
    tm = z_ref.shape[0]
    t = pl.program_id(1)

    @pl.when(t == 0)
    def _():
        zs[0:POOL_HALO, :] = zm_ref[...]
        cs[0, 0:CONV_HALO - N_META, :] = jnp.zeros((CONV_HALO - N_META, CONV_WIDTH), F32)
        cs[0, CONV_HALO - N_META:CONV_HALO, :] = am_ref[...]

    @pl.when(t > 0)
    def _():
        zs[0:POOL_HALO, :] = zs[tm:tm + POOL_HALO, :]
        cs[0, 0:CONV_HALO, :] = cs[0, tm:tm + CONV_HALO, :]

    zs[POOL_HALO:POOL_HALO + tm, :] = z_ref[...]
    cs[0, CONV_HALO:CONV_HALO + tm, :] = a_ref[...]

    n_rows = CONV_HALO + tm
    conv_in = cs[0]
    for s in range(1, SUBLANES):
        cs[s] = pltpu.roll(conv_in, n_rows - s, axis=0)

    ya_parts = []
    for g, w in enumerate(POOL_WINDOWS):
        cols = slice(g * POOL_GROUP_DIM, (g + 1) * POOL_GROUP_DIM)
        zz = zs[:, cols]
        win = zz
        shift = 1
        while shift < w:
            win = win + pltpu.roll(win, shift, axis=0)
            shift *= 2
        d = (win[POOL_HALO:] * (1.0 / w) - zz[POOL_HALO:]).astype(BF16)
        y = jnp.dot(d, wgrp_ref[g], preferred_element_type=F32)
        ya_parts.append((y * pscale_ref[:, cols]).astype(BF16))
    pooled = jnp.concatenate(ya_parts, axis=-1)
    y_a = jnp.dot(pooled, wpo_ref[...], preferred_element_type=F32)

    for cb in range(CONV_WIDTH // CONV_COLS):
        c0 = cb * CONV_COLS
        c1 = c0 + CONV_COLS
        for r0 in range(0, tm, CONV_ROWS):
            acc = jnp.broadcast_to(bdw_ref[:, c0:c1], (CONV_ROWS, CONV_COLS))
            for k in range(CONV_KERNEL):
                q, s = divmod(r0 + CONV_HALO - (CONV_KERNEL - 1) + k, SUBLANES)
                rows = slice(q * SUBLANES, q * SUBLANES + CONV_ROWS)
                wk = wdw_ref[k, :, c0:c1]
                acc = acc + cs[s, rows, c0:c1] * jnp.concatenate(
                    [wk] * (CONV_ROWS // SUBLANES), axis=0)
            conv_scr[r0:r0 + CONV_ROWS, c0:c1] = acc

    c = conv_scr[...]
    mu = jnp.mean(c, axis=-1, keepdims=True)
    cc = c - mu
    var = jnp.mean(cc * cc, axis=-1, keepdims=True)
    ln = cc * lax.rsqrt(var + LN_EPS) * lng_ref[...] + lnb_ref[...]
    act = (ln * jax.nn.sigmoid(ln)).astype(BF16)
    y_b = jnp.dot(act, wco_ref[...], preferred_element_type=F32)

    m = sga_ref[...].astype(F32) * y_a + sgb_ref[...].astype(F32) * y_b
    mo = jnp.dot(m.astype(BF16), wo_ref[...], preferred_element_type=F32)
    o_ref[...] = x_ref[...] + _rms_norm(mo, gpm_ref[...])


def _mixer(z, a, sga, sgb, x2d, z_meta, a_meta, w_grp, pool_scale, w_pool_out,
           w_dw, b_dw, ln_g, ln_b, w_conv_out, w_o, g_post, *, batch, seq):
    tm = MIX_TM
    nt = seq // tm
    row = lambda b, t: (b * nt + t, 0)
    const2 = lambda b, t: (0, 0)
    const3 = lambda b, t: (0, 0, 0)
    once = pl.Buffered(1)
    return pl.pallas_call(
        _mixer_kernel,
        out_shape=jax.ShapeDtypeStruct((batch * seq, D_MODEL), F32),
        grid=(batch, nt),
        in_specs=[
            pl.BlockSpec((tm, POOL_WIDTH), row),
            pl.BlockSpec((tm, CONV_WIDTH), row),
            pl.BlockSpec((tm, D_MODEL), row),
            pl.BlockSpec((tm, D_MODEL), row),
            pl.BlockSpec((tm, D_MODEL), row),
            pl.BlockSpec((N_META, POOL_WIDTH), const2, pipeline_mode=once),
            pl.BlockSpec((N_META, CONV_WIDTH), const2, pipeline_mode=once),
            pl.BlockSpec((len(POOL_WINDOWS), POOL_GROUP_DIM, POOL_GROUP_DIM), const3,
                         pipeline_mode=once),
            pl.BlockSpec((1, POOL_WIDTH), const2, pipeline_mode=once),
            pl.BlockSpec((POOL_WIDTH, D_MODEL), const2, pipeline_mode=once),
            pl.BlockSpec((CONV_KERNEL, SUBLANES, CONV_WIDTH), const3, pipeline_mode=once),
            pl.BlockSpec((1, CONV_WIDTH), const2, pipeline_mode=once),
            pl.BlockSpec((1, CONV_WIDTH), const2, pipeline_mode=once),
            pl.BlockSpec((1, CONV_WIDTH), const2, pipeline_mode=once),
            pl.BlockSpec((CONV_WIDTH, D_MODEL), const2, pipeline_mode=once),
            pl.BlockSpec((D_MODEL, D_MODEL), const2, pipeline_mode=once),
            pl.BlockSpec((1, D_MODEL), const2, pipeline_mode=once),
        ],
        out_specs=pl.BlockSpec((tm, D_MODEL), row),
        scratch_shapes=[
            pltpu.VMEM((POOL_HALO + tm, POOL_WIDTH), F32),
            pltpu.VMEM((SUBLANES, CONV_HALO + tm, CONV_WIDTH), F32),
            pltpu.VMEM((tm, CONV_WIDTH), F32),
        ],
        compiler_params=pltpu.CompilerParams(
            dimension_semantics=("arbitrary", "arbitrary"),
            vmem_limit_bytes=V7X_VMEM_LIMIT_BYTES),
        name="mixer",
    )(z, a, sga, sgb, x2d, z_meta, a_meta, w_grp, pool_scale, w_pool_out,
      w_dw, b_dw, ln_g, ln_b, w_conv_out, w_o, g_post)


def _mlp_kernel(h_ref, g1_ref, wup_ref, wdn_ref, g2_ref, o_ref, u_scr):
    k = pl.program_id(1)
    last = pl.num_programs(1) - 1
    tm = h_ref.shape[0]
    mb = tm // MLP_ROW_BLOCKS
    row_blocks = [slice(r0, r0 + mb) for r0 in range(0, tm, mb)]

    def partial_out(rb):
        up = jnp.dot(u_scr[rb, :], wup_ref[...], preferred_element_type=F32)
        act = jnp.square(jnp.maximum(up, 0.0)).astype(BF16)
        return jnp.dot(act, wdn_ref[...], preferred_element_type=F32)

    def norm_in(rb):
        u_scr[rb, :] = _rms_norm(h_ref[rb, :], g1_ref[...]).astype(BF16)

    @pl.when(k == 0)
    def _():
        norm_in(row_blocks[0])
        for i, rb in enumerate(row_blocks):
            if i + 1 < len(row_blocks):
                norm_in(row_blocks[i + 1])
            o_ref[rb, :] = partial_out(rb)

    @pl.when(jnp.logical_and(k > 0, k < last))
    def _():
        o_ref[...] += partial_out(slice(None))

    @pl.when(k == last)
    def _():
        for rb in row_blocks:
            f = o_ref[rb, :] + partial_out(rb)
            o_ref[rb, :] = h_ref[rb, :] + _rms_norm(f, g2_ref[...])


def _mlp(h, g_pre, w_up, w_down, g_post):
    m = h.shape[0]
    row = lambda i, k: (i, 0)
    const2 = lambda i, k: (0, 0)
    return pl.pallas_call(
        _mlp_kernel,
        out_shape=jax.ShapeDtypeStruct((m, D_MODEL), F32),
        grid=(m // MLP_TM, D_FF // MLP_TF),
        in_specs=[
            pl.BlockSpec((MLP_TM, D_MODEL), row),
            pl.BlockSpec((1, D_MODEL), const2),
            pl.BlockSpec((D_MODEL, MLP_TF), lambda i, k: (0, k)),
            pl.BlockSpec((MLP_TF, D_MODEL), lambda i, k: (k, 0)),
            pl.BlockSpec((1, D_MODEL), const2),
        ],
        out_specs=pl.BlockSpec((MLP_TM, D_MODEL), row),
        scratch_shapes=[pltpu.VMEM((MLP_TM, D_MODEL), BF16)],
        compiler_params=pltpu.CompilerParams(
            dimension_semantics=("arbitrary", "arbitrary"),
            vmem_limit_bytes=V7X_VMEM_LIMIT_BYTES),
        name="mlp",
    )(h, g_pre, w_up, w_down, g_post)


def kernel(x, meta, g_pre_mix, w_in, w_pool_grp, pool_scale, w_pool_out, w_dw, b_dw, conv_ln_g, conv_ln_b, w_conv_out, w_o, g_post_mix, g_pre_mlp, w_up, w_down, g_post_mlp):
    batch, seq, _ = x.shape
    depth = w_in.shape[0]
    h = x.reshape(batch * seq, D_MODEL)
    meta_h = meta.astype(x.dtype)
    assert depth == 1, "meta rows are only carried as first-layer causal history"
    for l in range(depth):
        w_in_l = w_in[l].astype(BF16)
        g_pre = g_pre_mix[l][None]
        z_meta, a_meta = _in_proj(meta_h, g_pre, w_in_l, tm=N_META, with_gates=False)
        z, a, sga, sgb = _in_proj(h, g_pre, w_in_l, tm=IN_TM, with_gates=True)
        h = _mixer(z, a, sga, sgb, h, z_meta, a_meta,
                   w_pool_grp[l].astype(BF16), pool_scale[l][None],
                   w_pool_out[l].astype(BF16),
                   jnp.broadcast_to(w_dw[l][:, None, :], (CONV_KERNEL, SUBLANES, CONV_WIDTH)),
                   b_dw[l][None],
                   conv_ln_g[l][None], conv_ln_b[l][None],
                   w_conv_out[l].astype(BF16), w_o[l].astype(BF16),
                   g_post_mix[l][None], batch=batch, seq=seq)
        h = _mlp(h, g_pre_mlp[l][None], w_up[l].astype(BF16),
                 w_down[l].astype(BF16), g_post_mlp[l][None])
    return h.reshape(batch, seq, D_MODEL)
```

```python
import functools

import jax
import jax.numpy as jnp
from jax import lax
from jax.experimental import pallas as pl
from jax.experimental.pallas import tpu as pltpu

D_MODEL = 2048
N_META = 16
POOL_WIDTH = D_MODEL // 2
POOL_WINDOWS = (2, 4, 8, 16)
POOL_GROUP_DIM = POOL_WIDTH // len(POOL_WINDOWS)
CONV_WIDTH = D_MODEL // 2
CONV_KERNEL = 31
D_FF = 4 * D_MODEL
IN_COLS = POOL_WIDTH + 2 * CONV_WIDTH + 2 * D_MODEL
RMS_EPS = 1e-6
LN_EPS = 1e-5

V7X_VMEM_LIMIT_BYTES = 56 * 1024 * 1024
SUBLANES = 8

F32 = jnp.float32
BF16 = jnp.bfloat16

IN_TN = 512
IN_TM = 1024
N_Z_STEPS = POOL_WIDTH // IN_TN
N_GLU_PAIRS = CONV_WIDTH // IN_TN
N_GATE_STEPS = D_MODEL // IN_TN
STEP_GLU = N_Z_STEPS
STEP_GA = STEP_GLU + 2 * N_GLU_PAIRS
STEP_GB = STEP_GA + N_GATE_STEPS
N_IN_STEPS_ALL = STEP_GB + N_GATE_STEPS

MIX_TM = 256
POOL_HALO = 16
CONV_HALO = 32
CONV_ROWS = 32
CONV_COLS = 256

MLP_TM = 512
MLP_TF = 1024
MLP_ROW_BLOCKS = 2


def _rms_norm(x, g):
    ms = jnp.mean(x * x, axis=-1, keepdims=True)
    return x * lax.rsqrt(ms + RMS_EPS) * g


def _sigmoid(x):
    return 0.5 * jnp.tanh(0.5 * x) + 0.5


def _glu_pair(j):
    r = j - STEP_GLU
    return r // 2, r % 2


def _w_in_block(j):
    pair, is_g = _glu_pair(j)
    glu_block = N_Z_STEPS + is_g * N_GLU_PAIRS + pair
    return jnp.where(jnp.logical_and(j >= STEP_GLU, j < STEP_GA), glu_block, j)


def _in_proj_kernel(x_ref, g_ref, w_ref, z_ref, a_ref, *rest, with_gates):
    if with_gates:
        sga_ref, sgb_ref, u_scr, v_scr = rest
    else:
        u_scr, v_scr = rest
    j = pl.program_id(1)
    is_glu = jnp.logical_and(j >= STEP_GLU, j < STEP_GA)
    is_g_half = _glu_pair(j)[1] == 1

    @pl.when(j == 0)
    def _():
        u_scr[...] = _rms_norm(x_ref[...], g_ref[...]).astype(BF16)

    def proj():
        return jnp.dot(u_scr[...], w_ref[...].astype(BF16), preferred_element_type=F32)

    @pl.when(j < STEP_GLU)
    def _():
        z_ref[...] = proj()

    @pl.when(jnp.logical_and(is_glu, jnp.logical_not(is_g_half)))
    def _():
        v_scr[...] = proj()

    @pl.when(jnp.logical_and(is_glu, is_g_half))
    def _():
        a_ref[...] = v_scr[...] * _sigmoid(proj())

    if with_gates:
        @pl.when(jnp.logical_and(j >= STEP_GA, j < STEP_GB))
        def _():
            sga_ref[...] = _sigmoid(proj()).astype(BF16)

        @pl.when(j >= STEP_GB)
        def _():
            sgb_ref[...] = _sigmoid(proj()).astype(BF16)


def _in_proj(x2d, g_pre, w_in, *, tm, with_gates):
    m = x2d.shape[0]
    n_steps = N_IN_STEPS_ALL if with_gates else STEP_GA
    row = lambda i, j: (i, 0)
    out_shape = [jax.ShapeDtypeStruct((m, POOL_WIDTH), F32),
                 jax.ShapeDtypeStruct((m, CONV_WIDTH), F32)]
    out_specs = [
        pl.BlockSpec((tm, IN_TN), lambda i, j: (i, jnp.clip(j, 0, N_Z_STEPS - 1))),
        pl.BlockSpec((tm, IN_TN), lambda i, j: (i, jnp.clip(_glu_pair(j)[0], 0, N_GLU_PAIRS - 1))),
    ]
    if with_gates:
        out_shape += [jax.ShapeDtypeStruct((m, D_MODEL), BF16)] * 2
        out_specs += [
            pl.BlockSpec((tm, IN_TN), lambda i, j: (i, jnp.clip(j - STEP_GA, 0, N_GATE_STEPS - 1))),
            pl.BlockSpec((tm, IN_TN), lambda i, j: (i, jnp.clip(j - STEP_GB, 0, N_GATE_STEPS - 1))),
        ]
    return pl.pallas_call(
        functools.partial(_in_proj_kernel, with_gates=with_gates),
        out_shape=out_shape,
        grid=(m // tm, n_steps),
        in_specs=[
            pl.BlockSpec((tm, D_MODEL), row),
            pl.BlockSpec((1, D_MODEL), lambda i, j: (0, 0)),
            pl.BlockSpec((D_MODEL, IN_TN), lambda i, j: (0, _w_in_block(j))),
        ],
        out_specs=out_specs,
        scratch_shapes=[pltpu.VMEM((tm, D_MODEL), BF16),
                        pltpu.VMEM((tm, IN_TN), F32)],
        compiler_params=pltpu.CompilerParams(
            dimension_semantics=("arbitrary", "arbitrary"),
            vmem_limit_bytes=V7X_VMEM_LIMIT_BYTES),
        name="in_proj_gates" if with_gates else "in_proj_meta",
    )(x2d, g_pre, w_in)


def _mixer_kernel(z_ref, a_ref, sga_ref, sgb_ref, x_ref, zm_ref, am_ref,
                  wgrp_ref, pscale_ref, wpo_ref, wdw_ref, bdw_ref,
                  lng_ref, lnb_ref, wco_ref, wo_ref, gpm_ref,
                  o_ref, zs, cs, conv_scr):
    tm = z_ref.shape[0]
    t = pl.program_id(1)

    @pl.when(t == 0)
    def _():
        zs[0:POOL_HALO, :] = zm_ref[...]
        cs[0, 0:CONV_HALO - N_META, :] = jnp.zeros((CONV_HALO - N_META, CONV_WIDTH), F32)
        cs[0, CONV_HALO - N_META:CONV_HALO, :] = am_ref[...]

    @pl.when(t > 0)
    def _():
        zs[0:POOL_HALO, :] = zs[tm:tm + POOL_HALO, :]
        cs[0, 0:CONV_HALO, :] = cs[0, tm:tm + CONV_HALO, :]

    zs[POOL_HALO:POOL_HALO + tm, :] = z_ref[...]
    cs[0, CONV_HALO:CONV_HALO + tm, :] = a_ref[...]

    n_rows = CONV_HALO + tm
    conv_in = cs[0]
    for s in range(1, SUBLANES):
        cs[s] = pltpu.roll(conv_in, n_rows - s, axis=0)

    ya_parts = []
    for g, w in enumerate(POOL_WINDOWS):
        cols = slice(g * POOL_GROUP_DIM, (g + 1) * POOL_GROUP_DIM)
        zz = zs[:, cols]
        win = zz
        shift = 1
        while shift < w:
            win = win + pltpu.roll(win, shift, axis=0)
            shift *= 2
        d = (win[POOL_HALO:] * (1.0 / w) - zz[POOL_HALO:]).astype(BF16)
        y = jnp.dot(d, wgrp_ref[g], preferred_element_type=F32)
        ya_parts.append((y * pscale_ref[:, cols]).astype(BF16))
    pooled = jnp.concatenate(ya_parts, axis=-1)
    y_a = jnp.dot(pooled, wpo_ref[...], preferred_element_type=F32)

    for cb in range(CONV_WIDTH // CONV_COLS):
        c0 = cb * CONV_COLS
        c1 = c0 + CONV_COLS
        for r0 in range(0, tm, CONV_ROWS):
            acc = jnp.broadcast_to(bdw_ref[:, c0:c1], (CONV_ROWS, CONV_COLS))
            for k in range(CONV_KERNEL):
                q, s = divmod(r0 + CONV_HALO - (CONV_KERNEL - 1) + k, SUBLANES)
                rows = slice(q * SUBLANES, q * SUBLANES + CONV_ROWS)
                wk = wdw_ref[k, :, c0:c1]
                acc = acc + cs[s, rows, c0:c1] * jnp.concatenate(
                    [wk] * (CONV_ROWS // SUBLANES), axis=0)
            conv_scr[r0:r0 + CONV_ROWS, c0:c1] = acc

    c = conv_scr[...]
    mu = jnp.mean(c, axis=-1, keepdims=True)
    cc = c - mu
    var = jnp.mean(cc * cc, axis=-1, keepdims=True)
    ln = cc * lax.rsqrt(var + LN_EPS) * lng_ref[...] + lnb_ref[...]
    act = (ln * _sigmoid(ln)).astype(BF16)
    y_b = jnp.dot(act, wco_ref[...], preferred_element_type=F32)

    m = sga_ref[...].astype(F32) * y_a + sgb_ref[...].astype(F32) * y_b
    mo = jnp.dot(m.astype(BF16), wo_ref[...], preferred_element_type=F32)
    o_ref[...] = x_ref[...] + _rms_norm(mo, gpm_ref[...])


def _mixer(z, a, sga, sgb, x2d, z_meta, a_meta, w_grp, pool_scale, w_pool_out,
           w_dw, b_dw, ln_g, ln_b, w_conv_out, w_o, g_post, *, batch, seq):
    tm = MIX_TM
    nt = seq // tm
    row = lambda b, t: (b * nt + t, 0)
    const2 = lambda b, t: (0, 0)
    const3 = lambda b, t: (0, 0, 0)
    once = pl.Buffered(1)
    return pl.pallas_call(
        _mixer_kernel,
        out_shape=jax.ShapeDtypeStruct((batch * seq, D_MODEL), F32),
        grid=(batch, nt),
        in_specs=[
            pl.BlockSpec((tm, POOL_WIDTH), row),
            pl.BlockSpec((tm, CONV_WIDTH), row),
            pl.BlockSpec((tm, D_MODEL), row),
            pl.BlockSpec((tm, D_MODEL), row),
            pl.BlockSpec((tm, D_MODEL), row),
            pl.BlockSpec((N_META, POOL_WIDTH), const2, pipeline_mode=once),
            pl.BlockSpec((N_META, CONV_WIDTH), const2, pipeline_mode=once),
            pl.BlockSpec((len(POOL_WINDOWS), POOL_GROUP_DIM, POOL_GROUP_DIM), const3,
                         pipeline_mode=once),
            pl.BlockSpec((1, POOL_WIDTH), const2, pipeline_mode=once),
            pl.BlockSpec((POOL_WIDTH, D_MODEL), const2, pipeline_mode=once),
            pl.BlockSpec((CONV_KERNEL, SUBLANES, CONV_WIDTH), const3, pipeline_mode=once),
            pl.BlockSpec((1, CONV_WIDTH), const2, pipeline_mode=once),
            pl.BlockSpec((1, CONV_WIDTH), const2, pipeline_mode=once),
            pl.BlockSpec((1, CONV_WIDTH), const2, pipeline_mode=once),
            pl.BlockSpec((CONV_WIDTH, D_MODEL), const2, pipeline_mode=once),
            pl.BlockSpec((D_MODEL, D_MODEL), const2, pipeline_mode=once),
            pl.BlockSpec((1, D_MODEL), const2, pipeline_mode=once),
        ],
        out_specs=pl.BlockSpec((tm, D_MODEL), row),
        scratch_shapes=[
            pltpu.VMEM((POOL_HALO + tm, POOL_WIDTH), F32),
            pltpu.VMEM((SUBLANES, CONV_HALO + tm, CONV_WIDTH), F32),
            pltpu.VMEM((tm, CONV_WIDTH), F32),
        ],
        compiler_params=pltpu.CompilerParams(
            dimension_semantics=("arbitrary", "arbitrary"),
            vmem_limit_bytes=V7X_VMEM_LIMIT_BYTES),
        name="mixer",
    )(z, a, sga, sgb, x2d, z_meta, a_meta, w_grp, pool_scale, w_pool_out,
      w_dw, b_dw, ln_g, ln_b, w_conv_out, w_o, g_post)


def _mlp_kernel(h_ref, g1_ref, wup_ref, wdn_ref, g2_ref, o_ref, u_scr):
    k = pl.program_id(1)
    last = pl.num_programs(1) - 1
    tm = h_ref.shape[0]
    mb = tm // MLP_ROW_BLOCKS
    row_blocks = [slice(r0, r0 + mb) for r0 in range(0, tm, mb)]

    def partial_out(rb):
        up = jnp.dot(u_scr[rb, :], wup_ref[...], preferred_element_type=F32)
        act = jnp.square(jnp.maximum(up, 0.0)).astype(BF16)
        return jnp.dot(act, wdn_ref[...], preferred_element_type=F32)

    def norm_in(rb):
        u_scr[rb, :] = _rms_norm(h_ref[rb, :], g1_ref[...]).astype(BF16)

    @pl.when(k == 0)
    def _():
        norm_in(row_blocks[0])
        for i, rb in enumerate(row_blocks):
            if i + 1 < len(row_blocks):
                norm_in(row_blocks[i + 1])
            o_ref[rb, :] = partial_out(rb)

    @pl.when(jnp.logical_and(k > 0, k < last))
    def _():
        o_ref[...] += partial_out(slice(None))

    @pl.when(k == last)
    def _():
        for rb in row_blocks:
            f = o_ref[rb, :] + partial_out(rb)
            o_ref[rb, :] = h_ref[rb, :] + _rms_norm(f, g2_ref[...])


def _mlp(h, g_pre, w_up, w_down, g_post):
    m = h.shape[0]
    row = lambda i, k: (i, 0)
    const2 = lambda i, k: (0, 0)
    return pl.pallas_call(
        _mlp_kernel,
        out_shape=jax.ShapeDtypeStruct((m, D_MODEL), F32),
        grid=(m // MLP_TM, D_FF // MLP_TF),
        in_specs=[
            pl.BlockSpec((MLP_TM, D_MODEL), row),
            pl.BlockSpec((1, D_MODEL), const2),
            pl.BlockSpec((D_MODEL, MLP_TF), lambda i, k: (0, k)),
            pl.BlockSpec((MLP_TF, D_MODEL), lambda i, k: (k, 0)),
            pl.BlockSpec((1, D_MODEL), const2),
        ],
        out_specs=pl.BlockSpec((MLP_TM, D_MODEL), row),
        scratch_shapes=[pltpu.VMEM((MLP_TM, D_MODEL), BF16)],
        compiler_params=pltpu.CompilerParams(
            dimension_semantics=("arbitrary", "arbitrary"),
            vmem_limit_bytes=V7X_VMEM_LIMIT_BYTES),
        name="mlp",
    )(h, g_pre, w_up, w_down, g_post)


def kernel(x, meta, g_pre_mix, w_in, w_pool_grp, pool_scale, w_pool_out, w_dw, b_dw, conv_ln_g, conv_ln_b, w_conv_out, w_o, g_post_mix, g_pre_mlp, w_up, w_down, g_post_mlp):
    batch, seq, _ = x.shape
    depth = w_in.shape[0]
    h = x.reshape(batch * seq, D_MODEL)
    meta_h = meta.astype(x.dtype)
    assert depth == 1, "meta rows are only carried as first-layer causal history"
    for l in range(depth):
        w_in_l = w_in[l]
        g_pre = g_pre_mix[l][None]
        z_meta, a_meta = _in_proj(meta_h, g_pre, w_in_l, tm=N_META, with_gates=False)
        z, a, sga, sgb = _in_proj(h, g_pre, w_in_l, tm=IN_TM, with_gates=True)
        h = _mixer(z, a, sga, sgb, h, z_meta, a_meta,
                   w_pool_grp[l].astype(BF16), pool_scale[l][None],
                   w_pool_out[l].astype(BF16),
                   jnp.broadcast_to(w_dw[l][:, None, :], (CONV_KERNEL, SUBLANES, CONV_WIDTH)),
                   b_dw[l][None],
                   conv_ln_g[l][None], conv_ln_b[l][None],
                   w_conv_out[l].astype(BF16), w_o[l].astype(BF16),
                   g_post_mix[l][None], batch=batch, seq=seq)
        h = _mlp(h, g_pre_mlp[l][None], w_up[l].astype(BF16),
                 w_down[l].astype(BF16), g_post_mlp[l][None])
    return h.reshape(batch, seq, D_MODEL)
```

```python
import jax
import jax.numpy as jnp
from jax import lax
from jax.experimental import pallas as pl
from jax.experimental.pallas import tpu as pltpu

D_MODEL = 2048
N_META = 16
POOL_WIDTH = D_MODEL // 2
POOL_WINDOWS = (2, 4, 8, 16)
POOL_GROUP_DIM = POOL_WIDTH // len(POOL_WINDOWS)
CONV_WIDTH = D_MODEL // 2
CONV_KERNEL = 31
D_FF = 4 * D_MODEL
IN_COLS = POOL_WIDTH + 2 * CONV_WIDTH + 2 * D_MODEL
RMS_EPS = 1e-6
LN_EPS = 1e-5

V7X_VMEM_LIMIT_BYTES = 56 * 1024 * 1024
SUBLANES = 8

F32 = jnp.float32
BF16 = jnp.bfloat16

IN_TM = 256
IN_CHUNK = 1024
COL_V = POOL_WIDTH
COL_G = COL_V + CONV_WIDTH
COL_GA = COL_G + CONV_WIDTH
COL_GB = COL_GA + D_MODEL

MIX_TM = 256
POOL_HALO = 16
CONV_HALO = 32
CONV_ROWS = 32
CONV_COLS = 256

MLP_TM = 512
MLP_TF = 1024
MLP_ROW_BLOCKS = 2


def _rms_norm(x, g):
    ms = jnp.mean(x * x, axis=-1, keepdims=True)
    return x * lax.rsqrt(ms + RMS_EPS) * g


def _sigmoid(x):
    return 0.5 * jnp.tanh(0.5 * x) + 0.5


def _in_proj_kernel(x_ref, g_ref, w_ref, z_ref, a_ref, *gate_refs):
    u = _rms_norm(x_ref[...], g_ref[...]).astype(BF16)

    def proj(c0):
        return jnp.dot(u, w_ref[:, c0:c0 + IN_CHUNK], preferred_element_type=F32)

    for c in range(0, POOL_WIDTH, IN_CHUNK):
        z_ref[:, c:c + IN_CHUNK] = proj(c)
    for c in range(0, CONV_WIDTH, IN_CHUNK):
        a_ref[:, c:c + IN_CHUNK] = proj(COL_V + c) * _sigmoid(proj(COL_G + c))
    for gate_ref, col0 in zip(gate_refs, (COL_GA, COL_GB)):
        for c in range(0, D_MODEL, IN_CHUNK):
            gate_ref[:, c:c + IN_CHUNK] = _sigmoid(proj(col0 + c)).astype(BF16)


def _in_proj(x2d, g_pre, w_in, *, tm, with_gates):
    m = x2d.shape[0]
    n_cols = IN_COLS if with_gates else COL_GA
    row = lambda i: (i, 0)
    const = lambda i: (0, 0)
    once = pl.Buffered(1)
    out_shape = [jax.ShapeDtypeStruct((m, POOL_WIDTH), F32),
                 jax.ShapeDtypeStruct((m, CONV_WIDTH), F32)]
    out_specs = [pl.BlockSpec((tm, POOL_WIDTH), row), pl.BlockSpec((tm, CONV_WIDTH), row)]
    if with_gates:
        out_shape += [jax.ShapeDtypeStruct((m, D_MODEL), BF16)] * 2
        out_specs += [pl.BlockSpec((tm, D_MODEL), row)] * 2
    return pl.pallas_call(
        _in_proj_kernel,
        out_shape=out_shape,
        grid=(m // tm,),
        in_specs=[
            pl.BlockSpec((tm, D_MODEL), row),
            pl.BlockSpec((1, D_MODEL), const, pipeline_mode=once),
            pl.BlockSpec((D_MODEL, n_cols), const, pipeline_mode=once),
        ],
        out_specs=out_specs,
        compiler_params=pltpu.CompilerParams(
            dimension_semantics=("arbitrary",),
            vmem_limit_bytes=V7X_VMEM_LIMIT_BYTES),
        name="in_proj_gates" if with_gates else "in_proj_meta",
    )(x2d, g_pre, w_in)


def _mixer_kernel(z_ref, a_ref, sga_ref, sgb_ref, x_ref, zm_ref, am_ref,
                  wgrp_ref, pscale_ref, wpo_ref, wdw_ref, bdw_ref,
                  lng_ref, lnb_ref, wco_ref, wo_ref, gpm_ref,
                  o_ref, zs, cs, conv_scr):
    tm = z_ref.shape[0]
    t = pl.program_id(1)

    @pl.when(t == 0)
    def _():
        zs[0:POOL_HALO, :] = zm_ref[...]
        cs[0, 0:CONV_HALO - N_META, :] = jnp.zeros((CONV_HALO - N_META, CONV_WIDTH), F32)
        cs[0, CONV_HALO - N_META:CONV_HALO, :] = am_ref[...]

    @pl.when(t > 0)
    def _():
        zs[0:POOL_HALO, :] = zs[tm:tm + POOL_HALO, :]
        cs[0, 0:CONV_HALO, :] = cs[0, tm:tm + CONV_HALO, :]

    zs[POOL_HALO:POOL_HALO + tm, :] = z_ref[...]
    cs[0, CONV_HALO:CONV_HALO + tm, :] = a_ref[...]

    n_rows = CONV_HALO + tm
    conv_in = cs[0]
    for s in range(1, SUBLANES):
        cs[s] = pltpu.roll(conv_in, n_rows - s, axis=0)

    ya_parts = []
    for g, w in enumerate(POOL_WINDOWS):
        cols = slice(g * POOL_GROUP_DIM, (g + 1) * POOL_GROUP_DIM)
        zz = zs[:, cols]
        win = zz
        shift = 1
        while shift < w:
            win = win + pltpu.roll(win, shift, axis=0)
            shift *= 2
        d = (win[POOL_HALO:] * (1.0 / w) - zz[POOL_HALO:]).astype(BF16)
        y = jnp.dot(d, wgrp_ref[g], preferred_element_type=F32)
        ya_parts.append((y * pscale_ref[:, cols]).astype(BF16))
    pooled = jnp.concatenate(ya_parts, axis=-1)
    y_a = jnp.dot(pooled, wpo_ref[...], preferred_element_type=F32)

    for cb in range(CONV_WIDTH // CONV_COLS):
        c0 = cb * CONV_COLS
        c1 = c0 + CONV_COLS
        for r0 in range(0, tm, CONV_ROWS):
            acc = jnp.broadcast_to(bdw_ref[:, c0:c1], (CONV_ROWS, CONV_COLS))
            for k in range(CONV_KERNEL):
                q, s = divmod(r0 + CONV_HALO - (CONV_KERNEL - 1) + k, SUBLANES)
                rows = slice(q * SUBLANES, q * SUBLANES + CONV_ROWS)
                wk = wdw_ref[k, :, c0:c1]
                acc = acc + cs[s, rows, c0:c1] * jnp.concatenate(
                    [wk] * (CONV_ROWS // SUBLANES), axis=0)
            conv_scr[r0:r0 + CONV_ROWS, c0:c1] = acc

    c = conv_scr[...]
    mu = jnp.mean(c, axis=-1, keepdims=True)
    cc = c - mu
    var = jnp.mean(cc * cc, axis=-1, keepdims=True)
    ln = cc * lax.rsqrt(var + LN_EPS) * lng_ref[...] + lnb_ref[...]
    act = (ln * _sigmoid(ln)).astype(BF16)
    y_b = jnp.dot(act, wco_ref[...], preferred_element_type=F32)

    m = sga_ref[...].astype(F32) * y_a + sgb_ref[...].astype(F32) * y_b
    mo = jnp.dot(m.astype(BF16), wo_ref[...], preferred_element_type=F32)
    o_ref[...] = x_ref[...] + _rms_norm(mo, gpm_ref[...])


def _mixer(z, a, sga, sgb, x2d, z_meta, a_meta, w_grp, pool_scale, w_pool_out,
           w_dw, b_dw, ln_g, ln_b, w_conv_out, w_o, g_post, *, batch, seq):
    tm = MIX_TM
    nt = seq // tm
    row = lambda b, t: (b * nt + t, 0)
    const2 = lambda b, t: (0, 0)
    const3 = lambda b, t: (0, 0, 0)
    once = pl.Buffered(1)
    return pl.pallas_call(
        _mixer_kernel,
        out_shape=jax.ShapeDtypeStruct((batch * seq, D_MODEL), F32),
        grid=(batch, nt),
        in_specs=[
            pl.BlockSpec((tm, POOL_WIDTH), row),
            pl.BlockSpec((tm, CONV_WIDTH), row),
            pl.BlockSpec((tm, D_MODEL), row),
            pl.BlockSpec((tm, D_MODEL), row),
            pl.BlockSpec((tm, D_MODEL), row),
            pl.BlockSpec((N_META, POOL_WIDTH), const2, pipeline_mode=once),
            pl.BlockSpec((N_META, CONV_WIDTH), const2, pipeline_mode=once),
            pl.BlockSpec((len(POOL_WINDOWS), POOL_GROUP_DIM, POOL_GROUP_DIM), const3,
                         pipeline_mode=once),
            pl.BlockSpec((1, POOL_WIDTH), const2, pipeline_mode=once),
            pl.BlockSpec((POOL_WIDTH, D_MODEL), const2, pipeline_mode=once),
            pl.BlockSpec((CONV_KERNEL, SUBLANES, CONV_WIDTH), const3, pipeline_mode=once),
            pl.BlockSpec((1, CONV_WIDTH), const2, pipeline_mode=once),
            pl.BlockSpec((1, CONV_WIDTH), const2, pipeline_mode=once),
            pl.BlockSpec((1, CONV_WIDTH), const2, pipeline_mode=once),
            pl.BlockSpec((CONV_WIDTH, D_MODEL), const2, pipeline_mode=once),
            pl.BlockSpec((D_MODEL, D_MODEL), const2, pipeline_mode=once),
            pl.BlockSpec((1, D_MODEL), const2, pipeline_mode=once),
        ],
        out_specs=pl.BlockSpec((tm, D_MODEL), row),
        scratch_shapes=[
            pltpu.VMEM((POOL_HALO + tm, POOL_WIDTH), F32),
            pltpu.VMEM((SUBLANES, CONV_HALO + tm, CONV_WIDTH), F32),
            pltpu.VMEM((tm, CONV_WIDTH), F32),
        ],
        compiler_params=pltpu.CompilerParams(
            dimension_semantics=("arbitrary", "arbitrary"),
            vmem_limit_bytes=V7X_VMEM_LIMIT_BYTES),
        name="mixer",
    )(z, a, sga, sgb, x2d, z_meta, a_meta, w_grp, pool_scale, w_pool_out,
      w_dw, b_dw, ln_g, ln_b, w_conv_out, w_o, g_post)


def _mlp_kernel(h_ref, g1_ref, wup_ref, wdn_ref, g2_ref, o_ref, u_scr):
    k = pl.program_id(1)
    last = pl.num_programs(1) - 1
    tm = h_ref.shape[0]
    mb = tm // MLP_ROW_BLOCKS
    row_blocks = [slice(r0, r0 + mb) for r0 in range(0, tm, mb)]

    def partial_out(rb):
        up = jnp.dot(u_scr[rb, :], wup_ref[...], preferred_element_type=F32)
        act = jnp.square(jnp.maximum(up, 0.0)).astype(BF16)
        return jnp.dot(act, wdn_ref[...], preferred_element_type=F32)

    def norm_in(rb):
        u_scr[rb, :] = _rms_norm(h_ref[rb, :], g1_ref[...]).astype(BF16)

    @pl.when(k == 0)
    def _():
        norm_in(row_blocks[0])
        for i, rb in enumerate(row_blocks):
            if i + 1 < len(row_blocks):
                norm_in(row_blocks[i + 1])
            o_ref[rb, :] = partial_out(rb)

    @pl.when(jnp.logical_and(k > 0, k < last))
    def _():
        o_ref[...] += partial_out(slice(None))

    @pl.when(k == last)
    def _():
        for rb in row_blocks:
            f = o_ref[rb, :] + partial_out(rb)
            o_ref[rb, :] = h_ref[rb, :] + _rms_norm(f, g2_ref[...])


def _mlp(h, g_pre, w_up, w_down, g_post):
    m = h.shape[0]
    row = lambda i, k: (i, 0)
    const2 = lambda i, k: (0, 0)
    return pl.pallas_call(
        _mlp_kernel,
        out_shape=jax.ShapeDtypeStruct((m, D_MODEL), F32),
        grid=(m // MLP_TM, D_FF // MLP_TF),
        in_specs=[
            pl.BlockSpec((MLP_TM, D_MODEL), row),
            pl.BlockSpec((1, D_MODEL), const2),
            pl.BlockSpec((D_MODEL, MLP_TF), lambda i, k: (0, k)),
            pl.BlockSpec((MLP_TF, D_MODEL), lambda i, k: (k, 0)),
            pl.BlockSpec((1, D_MODEL), const2),
        ],
        out_specs=pl.BlockSpec((MLP_TM, D_MODEL), row),
        scratch_shapes=[pltpu.VMEM((MLP_TM, D_MODEL), BF16)],
        compiler_params=pltpu.CompilerParams(
            dimension_semantics=("arbitrary", "arbitrary"),
            vmem_limit_bytes=V7X_VMEM_LIMIT_BYTES),
        name="mlp",
    )(h, g_pre, w_up, w_down, g_post)


def kernel(x, meta, g_pre_mix, w_in, w_pool_grp, pool_scale, w_pool_out, w_dw, b_dw, conv_ln_g, conv_ln_b, w_conv_out, w_o, g_post_mix, g_pre_mlp, w_up, w_down, g_post_mlp):
    batch, seq, _ = x.shape
    depth = w_in.shape[0]
    h = x.reshape(batch * seq, D_MODEL)
    meta_h = meta.astype(x.dtype)
    assert depth == 1, "meta rows are only carried as first-layer causal history"
    for l in range(depth):
        w_in_l = w_in[l].astype(BF16)
        g_pre = g_pre_mix[l][None]
        z_meta, a_meta = _in_proj(meta_h, g_pre, w_in_l, tm=N_META, with_gates=False)
        z, a, sga, sgb = _in_proj(h, g_pre, w_in_l, tm=IN_TM, with_gates=True)
        h = _mixer(z, a, sga, sgb, h, z_meta, a_meta,
                   w_pool_grp[l].astype(BF16), pool_scale[l][None],
                   w_pool_out[l].astype(BF16),
                   jnp.broadcast_to(w_dw[l][:, None, :], (CONV_KERNEL, SUBLANES, CONV_WIDTH)),
                   b_dw[l][None],
                   conv_ln_g[l][None], conv_ln_b[l][None],
                   w_conv_out[l].astype(BF16), w_o[l].astype(BF16),
                   g_post_mix[l][None], batch=batch, seq=seq)
        h = _mlp(h, g_pre_mlp[l][None], w_up[l].astype(BF16),
                 w_down[l].astype(BF16), g_post_mlp[l][None])
    return h.reshape(batch, seq, D_MODEL)
```

```python
import functools

import jax
import jax.numpy as jnp
from jax import lax
from jax.experimental import pallas as pl
from jax.experimental.pallas import tpu as pltpu

D_MODEL = 2048
N_META = 16
POOL_WIDTH = D_MODEL // 2
POOL_WINDOWS = (2, 4, 8, 16)
POOL_GROUP_DIM = POOL_WIDTH // len(POOL_WINDOWS)
CONV_WIDTH = D_MODEL // 2
CONV_KERNEL = 31
D_FF = 4 * D_MODEL
IN_COLS = POOL_WIDTH + 2 * CONV_WIDTH + 2 * D_MODEL
RMS_EPS = 1e-6
LN_EPS = 1e-5

V7X_VMEM_LIMIT_BYTES = 56 * 1024 * 1024
SUBLANES = 8

F32 = jnp.float32
BF16 = jnp.bfloat16

IN_TM = 256
IN_CHUNK = 1024
GATE_CHUNK = 512
COL_V = POOL_WIDTH
COL_G = COL_V + CONV_WIDTH
COL_GA = COL_G + CONV_WIDTH
COL_GB = COL_GA + D_MODEL

MIX_TM = 256
POOL_HALO = 16
CONV_HALO = 32
CONV_ROWS = 32
CONV_COLS = 256
CONV_HALF = CONV_WIDTH // 2
LN_ROWS = 64

MLP_TM = 512
MLP_TF = 1024
MLP_ROW_BLOCKS = 2


def _rms_norm(x, g):
    ms = jnp.mean(x * x, axis=-1, keepdims=True)
    return x * lax.rsqrt(ms + RMS_EPS) * g


def _sigmoid(x):
    return 0.5 * jnp.tanh(0.5 * x) + 0.5


def _meta_proj_kernel(x_ref, g_ref, w_ref, z_ref, a_ref):
    u = _rms_norm(x_ref[...], g_ref[...]).astype(BF16)

    def proj(c0):
        return jnp.dot(u, w_ref[:, c0:c0 + IN_CHUNK], preferred_element_type=F32)

    z_ref[...] = proj(0)
    a_ref[...] = proj(COL_V) * _sigmoid(proj(COL_G))


def _meta_proj(meta, g_pre, w_in):
    const = lambda i: (0, 0)
    return pl.pallas_call(
        _meta_proj_kernel,
        out_shape=[jax.ShapeDtypeStruct((N_META, POOL_WIDTH), F32),
                   jax.ShapeDtypeStruct((N_META, CONV_WIDTH), F32)],
        grid=(1,),
        in_specs=[
            pl.BlockSpec((N_META, D_MODEL), const),
            pl.BlockSpec((1, D_MODEL), const),
            pl.BlockSpec((D_MODEL, COL_GA), const),
        ],
        out_specs=[pl.BlockSpec((N_META, POOL_WIDTH), const),
                   pl.BlockSpec((N_META, CONV_WIDTH), const)],
        compiler_params=pltpu.CompilerParams(
            dimension_semantics=("arbitrary",),
            vmem_limit_bytes=V7X_VMEM_LIMIT_BYTES),
        name="meta_proj",
    )(meta, g_pre, w_in)


def _pool_items(zs, d_ref, tm):
    n_rows = POOL_HALO + tm

    def group(g, w):
        cols = slice(g * POOL_GROUP_DIM, (g + 1) * POOL_GROUP_DIM)
        zz = zs[:, cols]
        win = zz
        shift = 1
        while shift < w:
            win = win + pltpu.roll(win, shift, axis=0)
            shift *= 2
        d = win[POOL_HALO:n_rows] * (1.0 / w) - zz[POOL_HALO:n_rows]
        d_ref[:, cols] = d.astype(BF16)

    return [functools.partial(group, g, w) for g, w in enumerate(POOL_WINDOWS)]


def _shift_items(a_scr, sh_scr, col0, tm):
    n_rows = CONV_HALO + tm

    def copy(s):
        sh_scr[s - 1] = pltpu.roll(a_scr[:, col0:col0 + CONV_HALF], n_rows - s, axis=0)

    return [functools.partial(copy, s) for s in range(1, SUBLANES)]


def _conv_items(a_scr, sh_scr, wdw_ref, bdw_ref, conv_scr, col0, tm):
    def unit(r0, cb):
        lc = slice(cb * CONV_COLS, (cb + 1) * CONV_COLS)
        gc = slice(col0 + cb * CONV_COLS, col0 + (cb + 1) * CONV_COLS)
        acc = jnp.broadcast_to(bdw_ref[:, gc], (CONV_ROWS, CONV_COLS))
        for k in range(CONV_KERNEL):
            q, s = divmod(r0 + CONV_HALO - (CONV_KERNEL - 1) + k, SUBLANES)
            rows = slice(q * SUBLANES, q * SUBLANES + CONV_ROWS)
            src = a_scr[rows, gc] if s == 0 else sh_scr[s - 1, rows, lc]
            wk = wdw_ref[k, :, gc]
            acc = acc + src * jnp.concatenate([wk] * (CONV_ROWS // SUBLANES), axis=0)
        conv_scr[r0:r0 + CONV_ROWS, gc] = acc

    return [functools.partial(unit, r0, cb)
            for r0 in range(0, tm, CONV_ROWS)
            for cb in range(CONV_HALF // CONV_COLS)]


def _ln_swish_items(conv_scr, lng_ref, lnb_ref, act_ref, tm):
    def rows(r0):
        c = conv_scr[r0:r0 + LN_ROWS, :]
        mu = jnp.mean(c, axis=-1, keepdims=True)
        cc = c - mu
        var = jnp.mean(cc * cc, axis=-1, keepdims=True)
        ln = cc * lax.rsqrt(var + LN_EPS) * lng_ref[...] + lnb_ref[...]
        act_ref[r0:r0 + LN_ROWS, :] = (ln * _sigmoid(ln)).astype(BF16)

    return [functools.partial(rows, r0) for r0 in range(0, tm, LN_ROWS)]


def _interleave(dots, items):
    n = len(dots)
    done = 0
    for i, dot in enumerate(dots):
        dot()
        upto = (len(items) * (i + 1)) // n
        for item in items[done:upto]:
            item()
        done = upto


def _in_proj_kernel(x_ref, g_ref, w_ref, zm_ref, am_ref, wdw_ref, bdw_ref, lng_ref, lnb_ref,
                    d_ref, act_ref, sga_ref, sgb_ref, zs, a_scr, sh_scr, conv_scr):
    tm = x_ref.shape[0]
    t = pl.program_id(1)

    @pl.when(t == 0)
    def _():
        zs[0:POOL_HALO, :] = zm_ref[...]
        a_scr[0:CONV_HALO - N_META, :] = jnp.zeros((CONV_HALO - N_META, CONV_WIDTH), F32)
        a_scr[CONV_HALO - N_META:CONV_HALO, :] = am_ref[...]

    @pl.when(t > 0)
    def _():
        zs[0:POOL_HALO, :] = zs[tm:tm + POOL_HALO, :]
        a_scr[0:CONV_HALO, :] = a_scr[tm:tm + CONV_HALO, :]

    u = _rms_norm(x_ref[...], g_ref[...]).astype(BF16)

    def proj(c0, width):
        return jnp.dot(u, w_ref[:, c0:c0 + width], preferred_element_type=F32)

    zs[POOL_HALO:POOL_HALO + tm, :] = proj(0, IN_CHUNK)
    v = proj(COL_V, IN_CHUNK)
    pool = _pool_items(zs, d_ref, tm)
    for item in pool[:2]:
        item()
    a_scr[CONV_HALO:CONV_HALO + tm, :] = v * _sigmoid(proj(COL_G, IN_CHUNK))

    def gate_dot(out_ref, col0, c):
        out_ref[:, c:c + GATE_CHUNK] = _sigmoid(proj(col0 + c, GATE_CHUNK)).astype(BF16)

    dots = [functools.partial(gate_dot, out_ref, col0, c)
            for out_ref, col0 in ((sga_ref, COL_GA), (sgb_ref, COL_GB))
            for c in range(0, D_MODEL, GATE_CHUNK)]
    items = pool[2:]
    for col0 in range(0, CONV_WIDTH, CONV_HALF):
        items += _shift_items(a_scr, sh_scr, col0, tm)
        items += _conv_items(a_scr, sh_scr, wdw_ref, bdw_ref, conv_scr, col0, tm)
    items += _ln_swish_items(conv_scr, lng_ref, lnb_ref, act_ref, tm)
    _interleave(dots, items)


def _in_proj(x2d, g_pre, w_in, z_meta, a_meta, w_dw_b, b_dw, ln_g, ln_b, *, batch, seq):
    tm = IN_TM
    nt = seq // tm
    m = batch * seq
    row = lambda b, t: (b * nt + t, 0)
    const2 = lambda b, t: (0, 0)
    const3 = lambda b, t: (0, 0, 0)
    once = pl.Buffered(1)
    return pl.pallas_call(
        _in_proj_kernel,
        out_shape=[jax.ShapeDtypeStruct((m, POOL_WIDTH), BF16),
                   jax.ShapeDtypeStruct((m, CONV_WIDTH), BF16),
                   jax.ShapeDtypeStruct((m, D_MODEL), BF16),
                   jax.ShapeDtypeStruct((m, D_MODEL), BF16)],
        grid=(batch, nt),
        in_specs=[
            pl.BlockSpec((tm, D_MODEL), row),
            pl.BlockSpec((1, D_MODEL), const2, pipeline_mode=once),
            pl.BlockSpec((D_MODEL, IN_COLS), const2, pipeline_mode=once),
            pl.BlockSpec((N_META, POOL_WIDTH), const2, pipeline_mode=once),
            pl.BlockSpec((N_META, CONV_WIDTH), const2, pipeline_mode=once),
            pl.BlockSpec((CONV_KERNEL, SUBLANES, CONV_WIDTH), const3, pipeline_mode=once),
            pl.BlockSpec((1, CONV_WIDTH), const2, pipeline_mode=once),
            pl.BlockSpec((1, CONV_WIDTH), const2, pipeline_mode=once),
            pl.BlockSpec((1, CONV_WIDTH), const2, pipeline_mode=once),
        ],
        out_specs=[
            pl.BlockSpec((tm, POOL_WIDTH), row),
            pl.BlockSpec((tm, CONV_WIDTH), row),
            pl.BlockSpec((tm, D_MODEL), row),
            pl.BlockSpec((tm, D_MODEL), row),
        ],
        scratch_shapes=[
            pltpu.VMEM((POOL_HALO + tm, POOL_WIDTH), F32),
            pltpu.VMEM((CONV_HALO + tm, CONV_WIDTH), F32),
            pltpu.VMEM((SUBLANES - 1, CONV_HALO + tm, CONV_HALF), F32),
            pltpu.VMEM((tm, CONV_WIDTH), F32),
        ],
        compiler_params=pltpu.CompilerParams(
            dimension_semantics=("arbitrary", "arbitrary"),
            vmem_limit_bytes=V7X_VMEM_LIMIT_BYTES),
        name="in_proj",
    )(x2d, g_pre, w_in, z_meta, a_meta, w_dw_b, b_dw, ln_g, ln_b)


def _mixer_kernel(d_ref, act_ref, sga_ref, sgb_ref, x_ref,
                  wgrp_ref, pscale_ref, wpo_ref, wco_ref, wo_ref, gpm_ref, o_ref):
    ya_parts = []
    for g in range(len(POOL_WINDOWS)):
        cols = slice(g * POOL_GROUP_DIM, (g + 1) * POOL_GROUP_DIM)
        y = jnp.dot(d_ref[:, cols], wgrp_ref[g], preferred_element_type=F32)
        ya_parts.append((y * pscale_ref[:, cols]).astype(BF16))
    pooled = jnp.concatenate(ya_parts, axis=-1)
    y_a = jnp.dot(pooled, wpo_ref[...], preferred_element_type=F32)
    y_b = jnp.dot(act_ref[...], wco_ref[...], preferred_element_type=F32)
    m = sga_ref[...].astype(F32) * y_a + sgb_ref[...].astype(F32) * y_b
    mo = jnp.dot(m.astype(BF16), wo_ref[...], preferred_element_type=F32)
    o_ref[...] = x_ref[...] + _rms_norm(mo, gpm_ref[...])


def _mixer(d, act, sga, sgb, x2d, w_grp, pool_scale, w_pool_out, w_conv_out, w_o, g_post):
    tm = MIX_TM
    m = x2d.shape[0]
    row = lambda i: (i, 0)
    const2 = lambda i: (0, 0)
    const3 = lambda i: (0, 0, 0)
    once = pl.Buffered(1)
    return pl.pallas_call(
        _mixer_kernel,
        out_shape=jax.ShapeDtypeStruct((m, D_MODEL), F32),
        grid=(m // tm,),
        in_specs=[
            pl.BlockSpec((tm, POOL_WIDTH), row),
            pl.BlockSpec((tm, CONV_WIDTH), row),
            pl.BlockSpec((tm, D_MODEL), row),
            pl.BlockSpec((tm, D_MODEL), row),
            pl.BlockSpec((tm, D_MODEL), row),
            pl.BlockSpec((len(POOL_WINDOWS), POOL_GROUP_DIM, POOL_GROUP_DIM), const3,
                         pipeline_mode=once),
            pl.BlockSpec((1, POOL_WIDTH), const2, pipeline_mode=once),
            pl.BlockSpec((POOL_WIDTH, D_MODEL), const2, pipeline_mode=once),
            pl.BlockSpec((CONV_WIDTH, D_MODEL), const2, pipeline_mode=once),
            pl.BlockSpec((D_MODEL, D_MODEL), const2, pipeline_mode=once),
            pl.BlockSpec((1, D_MODEL), const2, pipeline_mode=once),
        ],
        out_specs=pl.BlockSpec((tm, D_MODEL), row),
        compiler_params=pltpu.CompilerParams(
            dimension_semantics=("arbitrary",),
            vmem_limit_bytes=V7X_VMEM_LIMIT_BYTES),
        name="mixer",
    )(d, act, sga, sgb, x2d, w_grp, pool_scale, w_pool_out, w_conv_out, w_o, g_post)


def _mlp_kernel(h_ref, g1_ref, wup_ref, wdn_ref, g2_ref, o_ref, u_scr):
    k = pl.program_id(1)
    last = pl.num_programs(1) - 1
    tm = h_ref.shape[0]
    mb = tm // MLP_ROW_BLOCKS
    row_blocks = [slice(r0, r0 + mb) for r0 in range(0, tm, mb)]

    def partial_out(rb):
        up = jnp.dot(u_scr[rb, :], wup_ref[...], preferred_element_type=F32)
        act = jnp.square(jnp.maximum(up, 0.0)).astype(BF16)
        return jnp.dot(act, wdn_ref[...], preferred_element_type=F32)

    def norm_in(rb):
        u_scr[rb, :] = _rms_norm(h_ref[rb, :], g1_ref[...]).astype(BF16)

    @pl.when(k == 0)
    def _():
        norm_in(row_blocks[0])
        for i, rb in enumerate(row_blocks):
            if i + 1 < len(row_blocks):
                norm_in(row_blocks[i + 1])
            o_ref[rb, :] = partial_out(rb)

    @pl.when(jnp.logical_and(k > 0, k < last))
    def _():
        o_ref[...] += partial_out(slice(None))

    @pl.when(k == last)
    def _():
        for rb in row_blocks:
            f = o_ref[rb, :] + partial_out(rb)
            o_ref[rb, :] = h_ref[rb, :] + _rms_norm(f, g2_ref[...])


def _mlp(h, g_pre, w_up, w_down, g_post):
    m = h.shape[0]
    row = lambda i, k: (i, 0)
    const2 = lambda i, k: (0, 0)
    return pl.pallas_call(
        _mlp_kernel,
        out_shape=jax.ShapeDtypeStruct((m, D_MODEL), F32),
        grid=(m // MLP_TM, D_FF // MLP_TF),
        in_specs=[
            pl.BlockSpec((MLP_TM, D_MODEL), row),
            pl.BlockSpec((1, D_MODEL), const2),
            pl.BlockSpec((D_MODEL, MLP_TF), lambda i, k: (0, k)),
            pl.BlockSpec((MLP_TF, D_MODEL), lambda i, k: (k, 0)),
            pl.BlockSpec((1, D_MODEL), const2),
        ],
        out_specs=pl.BlockSpec((MLP_TM, D_MODEL), row),
        scratch_shapes=[pltpu.VMEM((MLP_TM, D_MODEL), BF16)],
        compiler_params=pltpu.CompilerParams(
            dimension_semantics=("arbitrary", "arbitrary"),
            vmem_limit_bytes=V7X_VMEM_LIMIT_BYTES),
        name="mlp",
    )(h, g_pre, w_up, w_down, g_post)


def kernel(x, meta, g_pre_mix, w_in, w_pool_grp, pool_scale, w_pool_out, w_dw, b_dw, conv_ln_g, conv_ln_b, w_conv_out, w_o, g_post_mix, g_pre_mlp, w_up, w_down, g_post_mlp):
    batch, seq, _ = x.shape
    depth = w_in.shape[0]
    h = x.reshape(batch * seq, D_MODEL)
    meta_h = meta.astype(x.dtype)
    assert depth == 1, "meta rows are only carried as first-layer causal history"
    for l in range(depth):
        w_in_l = w_in[l].astype(BF16)
        g_pre = g_pre_mix[l][None]
        w_dw_b = jnp.broadcast_to(w_dw[l][:, None, :], (CONV_KERNEL, SUBLANES, CONV_WIDTH))
        z_meta, a_meta = _meta_proj(meta_h, g_pre, w_in_l)
        d, act, sga, sgb = _in_proj(h, g_pre, w_in_l, z_meta, a_meta, w_dw_b, b_dw[l][None],
                                    conv_ln_g[l][None], conv_ln_b[l][None], batch=batch, seq=seq)
        h = _mixer(d, act, sga, sgb, h, w_pool_grp[l].astype(BF16), pool_scale[l][None],
                   w_pool_out[l].astype(BF16), w_conv_out[l].astype(BF16), w_o[l].astype(BF16),
                   g_post_mix[l][None])
        h = _mlp(h, g_pre_mlp[l][None], w_up[l].astype(BF16),
                 w_down[l].astype(BF16), g_post_mlp[l][None])
    return h.reshape(batch, seq, D_MODEL)
```

```python
import jax
import jax.numpy as jnp
from jax import lax
from jax.experimental import pallas as pl
from jax.experimental.pallas import tpu as pltpu

D_MODEL = 2048
N_META = 16
POOL_WIDTH = D_MODEL // 2
POOL_WINDOWS = (2, 4, 8, 16)
POOL_GROUP_DIM = POOL_WIDTH // len(POOL_WINDOWS)
CONV_WIDTH = D_MODEL // 2
CONV_KERNEL = 31
D_FF = 4 * D_MODEL
IN_COLS = POOL_WIDTH + 2 * CONV_WIDTH + 2 * D_MODEL
RMS_EPS = 1e-6
LN_EPS = 1e-5

V7X_VMEM_LIMIT_BYTES = 56 * 1024 * 1024
SUBLANES = 8

F32 = jnp.float32
BF16 = jnp.bfloat16

IN_TM = 256
IN_CHUNK = 1024
COL_V = POOL_WIDTH
COL_G = COL_V + CONV_WIDTH
COL_GA = COL_G + CONV_WIDTH
COL_GB = COL_GA + D_MODEL

MIX_TM = 256
POOL_HALO = 16
CONV_HALO = 32
CONV_ROWS = 32
CONV_COLS = 256

MLP_TM = 512
MLP_TF = 1024
MLP_ROW_BLOCKS = 2


def _rms_norm(x, g):
    ms = jnp.mean(x * x, axis=-1, keepdims=True)
    return x * lax.rsqrt(ms + RMS_EPS) * g


def _sigmoid(x):
    return 0.5 * jnp.tanh(0.5 * x) + 0.5


def _meta_proj_kernel(x_ref, g_ref, w_ref, z_ref, a_ref):
    u = _rms_norm(x_ref[...], g_ref[...]).astype(BF16)

    def proj(c0):
        return jnp.dot(u, w_ref[:, c0:c0 + IN_CHUNK], preferred_element_type=F32)

    z_ref[...] = proj(0)
    a_ref[...] = proj(COL_V) * _sigmoid(proj(COL_G))


def _meta_proj(meta, g_pre, w_in):
    const = lambda i: (0, 0)
    return pl.pallas_call(
        _meta_proj_kernel,
        out_shape=[jax.ShapeDtypeStruct((N_META, POOL_WIDTH), F32),
                   jax.ShapeDtypeStruct((N_META, CONV_WIDTH), F32)],
        grid=(1,),
        in_specs=[
            pl.BlockSpec((N_META, D_MODEL), const),
            pl.BlockSpec((1, D_MODEL), const),
            pl.BlockSpec((D_MODEL, COL_GA), const),
        ],
        out_specs=[pl.BlockSpec((N_META, POOL_WIDTH), const),
                   pl.BlockSpec((N_META, CONV_WIDTH), const)],
        compiler_params=pltpu.CompilerParams(
            dimension_semantics=("arbitrary",),
            vmem_limit_bytes=V7X_VMEM_LIMIT_BYTES),
        name="meta_proj",
    )(meta, g_pre, w_in)


def _in_proj_kernel(x_ref, g_ref, w_ref, *refs):
    n_cast = (len(refs) - 4) // 2
    cast_in = refs[:n_cast]
    z_ref, a_ref, sga_ref, sgb_ref = refs[n_cast:n_cast + 4]
    cast_out = refs[n_cast + 4:]

    u = _rms_norm(x_ref[...], g_ref[...]).astype(BF16)

    def proj(c0):
        return jnp.dot(u, w_ref[:, c0:c0 + IN_CHUNK], preferred_element_type=F32)

    for c in range(0, POOL_WIDTH, IN_CHUNK):
        z_ref[:, c:c + IN_CHUNK] = proj(c)
    for c in range(0, CONV_WIDTH, IN_CHUNK):
        a_ref[:, c:c + IN_CHUNK] = proj(COL_V + c) * _sigmoid(proj(COL_G + c))
    for gate_ref, col0 in ((sga_ref, COL_GA), (sgb_ref, COL_GB)):
        for c in range(0, D_MODEL, IN_CHUNK):
            gate_ref[:, c:c + IN_CHUNK] = _sigmoid(proj(col0 + c)).astype(BF16)

    for src, dst in zip(cast_in, cast_out):
        dst[...] = src[...].astype(BF16)


def _in_proj(x2d, g_pre, w_in, later_weights):
    tm = IN_TM
    m = x2d.shape[0]
    n_steps = m // tm
    row = lambda i: (i, 0)
    const = lambda i: (0, 0)
    once = pl.Buffered(1)
    cast_specs = [pl.BlockSpec((w.shape[0] // n_steps, w.shape[1]), row) for w in later_weights]
    outs = pl.pallas_call(
        _in_proj_kernel,
        out_shape=[jax.ShapeDtypeStruct((m, POOL_WIDTH), F32),
                   jax.ShapeDtypeStruct((m, CONV_WIDTH), F32),
                   jax.ShapeDtypeStruct((m, D_MODEL), BF16),
                   jax.ShapeDtypeStruct((m, D_MODEL), BF16)]
                  + [jax.ShapeDtypeStruct(w.shape, BF16) for w in later_weights],
        grid=(n_steps,),
        in_specs=[
            pl.BlockSpec((tm, D_MODEL), row),
            pl.BlockSpec((1, D_MODEL), const, pipeline_mode=once),
            pl.BlockSpec((D_MODEL, IN_COLS), const, pipeline_mode=once),
        ] + cast_specs,
        out_specs=[pl.BlockSpec((tm, POOL_WIDTH), row), pl.BlockSpec((tm, CONV_WIDTH), row),
                   pl.BlockSpec((tm, D_MODEL), row), pl.BlockSpec((tm, D_MODEL), row)] + cast_specs,
        compiler_params=pltpu.CompilerParams(
            dimension_semantics=("arbitrary",),
            vmem_limit_bytes=V7X_VMEM_LIMIT_BYTES),
        name="in_proj",
    )(x2d, g_pre, w_in, *later_weights)
    return outs[:4], outs[4:]


def _mixer_kernel(z_ref, a_ref, sga_ref, sgb_ref, x_ref, zm_ref, am_ref,
                  wgrp_ref, pscale_ref, wpo_ref, wdw_ref, bdw_ref,
                  lng_ref, lnb_ref, wco_ref, wo_ref, gpm_ref,
                  o_ref, zs, cs, conv_scr):
    tm = z_ref.shape[0]
    t = pl.program_id(1)

    @pl.when(t == 0)
    def _():
        zs[0:POOL_HALO, :] = zm_ref[...]
        cs[0, 0:CONV_HALO - N_META, :] = jnp.zeros((CONV_HALO - N_META, CONV_WIDTH), F32)
        cs[0, CONV_HALO - N_META:CONV_HALO, :] = am_ref[...]

    @pl.when(t > 0)
    def _():
        zs[0:POOL_HALO, :] = zs[tm:tm + POOL_HALO, :]
        cs[0, 0:CONV_HALO, :] = cs[0, tm:tm + CONV_HALO, :]

    zs[POOL_HALO:POOL_HALO + tm, :] = z_ref[...]
    cs[0, CONV_HALO:CONV_HALO + tm, :] = a_ref[...]

    n_rows = CONV_HALO + tm
    conv_in = cs[0]
    for s in range(1, SUBLANES):
        cs[s] = pltpu.roll(conv_in, n_rows - s, axis=0)

    ya_parts = []
    for g, w in enumerate(POOL_WINDOWS):
        cols = slice(g * POOL_GROUP_DIM, (g + 1) * POOL_GROUP_DIM)
        zz = zs[:, cols]
        win = zz
        shift = 1
        while shift < w:
            win = win + pltpu.roll(win, shift, axis=0)
            shift *= 2
        d = (win[POOL_HALO:] * (1.0 / w) - zz[POOL_HALO:]).astype(BF16)
        y = jnp.dot(d, wgrp_ref[g], preferred_element_type=F32)
        ya_parts.append((y * pscale_ref[:, cols]).astype(BF16))
    pooled = jnp.concatenate(ya_parts, axis=-1)
    y_a = jnp.dot(pooled, wpo_ref[...], preferred_element_type=F32)

    for cb in range(CONV_WIDTH // CONV_COLS):
        c0 = cb * CONV_COLS
        c1 = c0 + CONV_COLS
        for r0 in range(0, tm, CONV_ROWS):
            acc = jnp.broadcast_to(bdw_ref[:, c0:c1], (CONV_ROWS, CONV_COLS))
            for k in range(CONV_KERNEL):
                q, s = divmod(r0 + CONV_HALO - (CONV_KERNEL - 1) + k, SUBLANES)
                rows = slice(q * SUBLANES, q * SUBLANES + CONV_ROWS)
                wk = wdw_ref[k, :, c0:c1]
                acc = acc + cs[s, rows, c0:c1] * jnp.concatenate(
                    [wk] * (CONV_ROWS // SUBLANES), axis=0)
            conv_scr[r0:r0 + CONV_ROWS, c0:c1] = acc

    c = conv_scr[...]
    mu = jnp.mean(c, axis=-1, keepdims=True)
    cc = c - mu
    var = jnp.mean(cc * cc, axis=-1, keepdims=True)
    ln = cc * lax.rsqrt(var + LN_EPS) * lng_ref[...] + lnb_ref[...]
    act = (ln * _sigmoid(ln)).astype(BF16)
    y_b = jnp.dot(act, wco_ref[...], preferred_element_type=F32)

    m = sga_ref[...].astype(F32) * y_a + sgb_ref[...].astype(F32) * y_b
    mo = jnp.dot(m.astype(BF16), wo_ref[...], preferred_element_type=F32)
    o_ref[...] = x_ref[...] + _rms_norm(mo, gpm_ref[...])


def _mixer(z, a, sga, sgb, x2d, z_meta, a_meta, w_grp, pool_scale, w_pool_out,
           w_dw, b_dw, ln_g, ln_b, w_conv_out, w_o, g_post, *, batch, seq):
    tm = MIX_TM
    nt = seq // tm
    row = lambda b, t: (b * nt + t, 0)
    const2 = lambda b, t: (0, 0)
    const3 = lambda b, t: (0, 0, 0)
    once = pl.Buffered(1)
    return pl.pallas_call(
        _mixer_kernel,
        out_shape=jax.ShapeDtypeStruct((batch * seq, D_MODEL), F32),
        grid=(batch, nt),
        in_specs=[
            pl.BlockSpec((tm, POOL_WIDTH), row),
            pl.BlockSpec((tm, CONV_WIDTH), row),
            pl.BlockSpec((tm, D_MODEL), row),
            pl.BlockSpec((tm, D_MODEL), row),
            pl.BlockSpec((tm, D_MODEL), row),
            pl.BlockSpec((N_META, POOL_WIDTH), const2, pipeline_mode=once),
            pl.BlockSpec((N_META, CONV_WIDTH), const2, pipeline_mode=once),
            pl.BlockSpec((len(POOL_WINDOWS), POOL_GROUP_DIM, POOL_GROUP_DIM), const3,
                         pipeline_mode=once),
            pl.BlockSpec((1, POOL_WIDTH), const2, pipeline_mode=once),
            pl.BlockSpec((POOL_WIDTH, D_MODEL), const2, pipeline_mode=once),
            pl.BlockSpec((CONV_KERNEL, SUBLANES, CONV_WIDTH), const3, pipeline_mode=once),
            pl.BlockSpec((1, CONV_WIDTH), const2, pipeline_mode=once),
            pl.BlockSpec((1, CONV_WIDTH), const2, pipeline_mode=once),
            pl.BlockSpec((1, CONV_WIDTH), const2, pipeline_mode=once),
            pl.BlockSpec((CONV_WIDTH, D_MODEL), const2, pipeline_mode=once),
            pl.BlockSpec((D_MODEL, D_MODEL), const2, pipeline_mode=once),
            pl.BlockSpec((1, D_MODEL), const2, pipeline_mode=once),
        ],
        out_specs=pl.BlockSpec((tm, D_MODEL), row),
        scratch_shapes=[
            pltpu.VMEM((POOL_HALO + tm, POOL_WIDTH), F32),
            pltpu.VMEM((SUBLANES, CONV_HALO + tm, CONV_WIDTH), F32),
            pltpu.VMEM((tm, CONV_WIDTH), F32),
        ],
        compiler_params=pltpu.CompilerParams(
            dimension_semantics=("arbitrary", "arbitrary"),
            vmem_limit_bytes=V7X_VMEM_LIMIT_BYTES),
        name="mixer",
    )(z, a, sga, sgb, x2d, z_meta, a_meta, w_grp, pool_scale, w_pool_out,
      w_dw, b_dw, ln_g, ln_b, w_conv_out, w_o, g_post)


def _mlp_kernel(h_ref, g1_ref, wup_ref, wdn_ref, g2_ref, o_ref, u_scr):
    k = pl.program_id(1)
    last = pl.num_programs(1) - 1
    tm = h_ref.shape[0]
    mb = tm // MLP_ROW_BLOCKS
    row_blocks = [slice(r0, r0 + mb) for r0 in range(0, tm, mb)]

    def partial_out(rb):
        up = jnp.dot(u_scr[rb, :], wup_ref[...], preferred_element_type=F32)
        act = jnp.square(jnp.maximum(up, 0.0)).astype(BF16)
        return jnp.dot(act, wdn_ref[...], preferred_element_type=F32)

    def norm_in(rb):
        u_scr[rb, :] = _rms_norm(h_ref[rb, :], g1_ref[...]).astype(BF16)

    @pl.when(k == 0)
    def _():
        norm_in(row_blocks[0])
        for i, rb in enumerate(row_blocks):
            if i + 1 < len(row_blocks):
                norm_in(row_blocks[i + 1])
            o_ref[rb, :] = partial_out(rb)

    @pl.when(jnp.logical_and(k > 0, k < last))
    def _():
        o_ref[...] += partial_out(slice(None))

    @pl.when(k == last)
    def _():
        for rb in row_blocks:
            f = o_ref[rb, :] + partial_out(rb)
            o_ref[rb, :] = h_ref[rb, :] + _rms_norm(f, g2_ref[...])


def _mlp(h, g_pre, w_up, w_down, g_post):
    m = h.shape[0]
    row = lambda i, k: (i, 0)
    const2 = lambda i, k: (0, 0)
    return pl.pallas_call(
        _mlp_kernel,
        out_shape=jax.ShapeDtypeStruct((m, D_MODEL), F32),
        grid=(m // MLP_TM, D_FF // MLP_TF),
        in_specs=[
            pl.BlockSpec((MLP_TM, D_MODEL), row),
            pl.BlockSpec((1, D_MODEL), const2),
            pl.BlockSpec((D_MODEL, MLP_TF), lambda i, k: (0, k)),
            pl.BlockSpec((MLP_TF, D_MODEL), lambda i, k: (k, 0)),
            pl.BlockSpec((1, D_MODEL), const2),
        ],
        out_specs=pl.BlockSpec((MLP_TM, D_MODEL), row),
        scratch_shapes=[pltpu.VMEM((MLP_TM, D_MODEL), BF16)],
        compiler_params=pltpu.CompilerParams(
            dimension_semantics=("arbitrary", "arbitrary"),
            vmem_limit_bytes=V7X_VMEM_LIMIT_BYTES),
        name="mlp",
    )(h, g_pre, w_up, w_down, g_post)


def kernel(x, meta, g_pre_mix, w_in, w_pool_grp, pool_scale, w_pool_out, w_dw, b_dw, conv_ln_g, conv_ln_b, w_conv_out, w_o, g_post_mix, g_pre_mlp, w_up, w_down, g_post_mlp):
    batch, seq, _ = x.shape
    depth = w_in.shape[0]
    h = x.reshape(batch * seq, D_MODEL)
    meta_h = meta.astype(x.dtype)
    assert depth == 1, "meta rows are only carried as first-layer causal history"
    for l in range(depth):
        w_in_l = w_in[l].astype(BF16)
        g_pre = g_pre_mix[l][None]
        z_meta, a_meta = _meta_proj(meta_h, g_pre, w_in_l)
        (z, a, sga, sgb), (w_pool_out_b, w_conv_out_b, w_o_b, w_up_b, w_down_b) = _in_proj(
            h, g_pre, w_in_l, [w_pool_out[l], w_conv_out[l], w_o[l], w_up[l], w_down[l]])
        h = _mixer(z, a, sga, sgb, h, z_meta, a_meta,
                   w_pool_grp[l].astype(BF16), pool_scale[l][None], w_pool_out_b,
                   jnp.broadcast_to(w_dw[l][:, None, :], (CONV_KERNEL, SUBLANES, CONV_WIDTH)),
                   b_dw[l][None], conv_ln_g[l][None], conv_ln_b[l][None],
                   w_conv_out_b, w_o_b, g_post_mix[l][None], batch=batch, seq=seq)
        h = _mlp(h, g_pre_mlp[l][None], w_up_b, w_down_b, g_post_mlp[l][None])
    return h.reshape(batch, seq, D_MODEL)
```

```python
import jax
import jax.numpy as jnp
from jax import lax
from jax.experimental import pallas as pl
from jax.experimental.pallas import tpu as pltpu

D_MODEL = 2048
N_META = 16
POOL_WIDTH = D_MODEL // 2
POOL_WINDOWS = (2, 4, 8, 16)
POOL_GROUP_DIM = POOL_WIDTH // len(POOL_WINDOWS)
CONV_WIDTH = D_MODEL // 2
CONV_KERNEL = 31
D_FF = 4 * D_MODEL
IN_COLS = POOL_WIDTH + 2 * CONV_WIDTH + 2 * D_MODEL
RMS_EPS = 1e-6
LN_EPS = 1e-5

V7X_VMEM_LIMIT_BYTES = 56 * 1024 * 1024
SUBLANES = 8

F32 = jnp.float32
BF16 = jnp.bfloat16

IN_TM = 256
IN_CHUNK = 1024
COL_V = POOL_WIDTH
COL_G = COL_V + CONV_WIDTH
COL_GA = COL_G + CONV_WIDTH
COL_GB = COL_GA + D_MODEL

MIX_TM = 256
POOL_HALO = 16
CONV_HALO = 32
CONV_ROWS = 32
CONV_COLS = 256

MLP_TM = 512
MLP_TF = 2048
MLP_ROW_BLOCKS = 2


def _rms_norm(x, g):
    ms = jnp.mean(x * x, axis=-1, keepdims=True)
    return x * lax.rsqrt(ms + RMS_EPS) * g


def _sigmoid(x):
    return 0.5 * jnp.tanh(0.5 * x) + 0.5


def _meta_proj_kernel(x_ref, g_ref, w_ref, z_ref, a_ref):
    u = _rms_norm(x_ref[...], g_ref[...]).astype(BF16)

    def proj(c0):
        return jnp.dot(u, w_ref[:, c0:c0 + IN_CHUNK], preferred_element_type=F32)

    z_ref[...] = proj(0)
    a_ref[...] = proj(COL_V) * _sigmoid(proj(COL_G))


def _meta_proj(meta, g_pre, w_in):
    const = lambda i: (0, 0)
    return pl.pallas_call(
        _meta_proj_kernel,
        out_shape=[jax.ShapeDtypeStruct((N_META, POOL_WIDTH), F32),
                   jax.ShapeDtypeStruct((N_META, CONV_WIDTH), F32)],
        grid=(1,),
        in_specs=[
            pl.BlockSpec((N_META, D_MODEL), const),
            pl.BlockSpec((1, D_MODEL), const),
            pl.BlockSpec((D_MODEL, COL_GA), const),
        ],
        out_specs=[pl.BlockSpec((N_META, POOL_WIDTH), const),
                   pl.BlockSpec((N_META, CONV_WIDTH), const)],
        compiler_params=pltpu.CompilerParams(
            dimension_semantics=("arbitrary",),
            vmem_limit_bytes=V7X_VMEM_LIMIT_BYTES),
        name="meta_proj",
    )(meta, g_pre, w_in)


def _in_proj_kernel(x_ref, g_ref, w_ref, *refs):
    n_cast = (len(refs) - 4) // 2
    cast_in = refs[:n_cast]
    z_ref, a_ref, sga_ref, sgb_ref = refs[n_cast:n_cast + 4]
    cast_out = refs[n_cast + 4:]

    u = _rms_norm(x_ref[...], g_ref[...]).astype(BF16)

    def proj(c0):
        return jnp.dot(u, w_ref[:, c0:c0 + IN_CHUNK], preferred_element_type=F32)

    for c in range(0, POOL_WIDTH, IN_CHUNK):
        z_ref[:, c:c + IN_CHUNK] = proj(c)
    for c in range(0, CONV_WIDTH, IN_CHUNK):
        a_ref[:, c:c + IN_CHUNK] = proj(COL_V + c) * _sigmoid(proj(COL_G + c))
    for gate_ref, col0 in ((sga_ref, COL_GA), (sgb_ref, COL_GB)):
        for c in range(0, D_MODEL, IN_CHUNK):
            gate_ref[:, c:c + IN_CHUNK] = _sigmoid(proj(col0 + c)).astype(BF16)

    for src, dst in zip(cast_in, cast_out):
        dst[...] = src[...].astype(BF16)


def _in_proj(x2d, g_pre, w_in, later_weights):
    tm = IN_TM
    m = x2d.shape[0]
    n_steps = m // tm
    row = lambda i: (i, 0)
    const = lambda i: (0, 0)
    once = pl.Buffered(1)
    cast_specs = [pl.BlockSpec((w.shape[0] // n_steps, w.shape[1]), row) for w in later_weights]
    outs = pl.pallas_call(
        _in_proj_kernel,
        out_shape=[jax.ShapeDtypeStruct((m, POOL_WIDTH), F32),
                   jax.ShapeDtypeStruct((m, CONV_WIDTH), F32),
                   jax.ShapeDtypeStruct((m, D_MODEL), BF16),
                   jax.ShapeDtypeStruct((m, D_MODEL), BF16)]
                  + [jax.ShapeDtypeStruct(w.shape, BF16) for w in later_weights],
        grid=(n_steps,),
        in_specs=[
            pl.BlockSpec((tm, D_MODEL), row),
            pl.BlockSpec((1, D_MODEL), const, pipeline_mode=once),
            pl.BlockSpec((D_MODEL, IN_COLS), const, pipeline_mode=once),
        ] + cast_specs,
        out_specs=[pl.BlockSpec((tm, POOL_WIDTH), row), pl.BlockSpec((tm, CONV_WIDTH), row),
                   pl.BlockSpec((tm, D_MODEL), row), pl.BlockSpec((tm, D_MODEL), row)] + cast_specs,
        compiler_params=pltpu.CompilerParams(
            dimension_semantics=("arbitrary",),
            vmem_limit_bytes=V7X_VMEM_LIMIT_BYTES),
        name="in_proj",
    )(x2d, g_pre, w_in, *later_weights)
    return outs[:4], outs[4:]


def _mixer_kernel(z_ref, a_ref, sga_ref, sgb_ref, x_ref, zm_ref, am_ref,
                  wgrp_ref, pscale_ref, wpo_ref, wdw_ref, bdw_ref,
                  lng_ref, lnb_ref, wco_ref, wo_ref, gpm_ref,
                  o_ref, zs, cs, conv_scr):
    tm = z_ref.shape[0]
    t = pl.program_id(1)

    @pl.when(t == 0)
    def _():
        zs[0:POOL_HALO, :] = zm_ref[...]
        cs[0, 0:CONV_HALO - N_META, :] = jnp.zeros((CONV_HALO - N_META, CONV_WIDTH), F32)
        cs[0, CONV_HALO - N_META:CONV_HALO, :] = am_ref[...]

    @pl.when(t > 0)
    def _():
        zs[0:POOL_HALO, :] = zs[tm:tm + POOL_HALO, :]
        cs[0, 0:CONV_HALO, :] = cs[0, tm:tm + CONV_HALO, :]

    zs[POOL_HALO:POOL_HALO + tm, :] = z_ref[...]
    cs[0, CONV_HALO:CONV_HALO + tm, :] = a_ref[...]

    n_rows = CONV_HALO + tm
    conv_in = cs[0]
    for s in range(1, SUBLANES):
        cs[s] = pltpu.roll(conv_in, n_rows - s, axis=0)

    ya_parts = []
    for g, w in enumerate(POOL_WINDOWS):
        cols = slice(g * POOL_GROUP_DIM, (g + 1) * POOL_GROUP_DIM)
        zz = zs[:, cols]
        win = zz
        shift = 1
        while shift < w:
            win = win + pltpu.roll(win, shift, axis=0)
            shift *= 2
        d = (win[POOL_HALO:] * (1.0 / w) - zz[POOL_HALO:]).astype(BF16)
        y = jnp.dot(d, wgrp_ref[g], preferred_element_type=F32)
        ya_parts.append((y * pscale_ref[:, cols]).astype(BF16))
    pooled = jnp.concatenate(ya_parts, axis=-1)
    y_a = jnp.dot(pooled, wpo_ref[...], preferred_element_type=F32)

    for cb in range(CONV_WIDTH // CONV_COLS):
        c0 = cb * CONV_COLS
        c1 = c0 + CONV_COLS
        for r0 in range(0, tm, CONV_ROWS):
            acc = jnp.broadcast_to(bdw_ref[:, c0:c1], (CONV_ROWS, CONV_COLS))
            for k in range(CONV_KERNEL):
                q, s = divmod(r0 + CONV_HALO - (CONV_KERNEL - 1) + k, SUBLANES)
                rows = slice(q * SUBLANES, q * SUBLANES + CONV_ROWS)
                wk = wdw_ref[k, :, c0:c1]
                acc = acc + cs[s, rows, c0:c1] * jnp.concatenate(
                    [wk] * (CONV_ROWS // SUBLANES), axis=0)
            conv_scr[r0:r0 + CONV_ROWS, c0:c1] = acc

    c = conv_scr[...]
    mu = jnp.mean(c, axis=-1, keepdims=True)
    cc = c - mu
    var = jnp.mean(cc * cc, axis=-1, keepdims=True)
    ln = cc * lax.rsqrt(var + LN_EPS) * lng_ref[...] + lnb_ref[...]
    act = (ln * _sigmoid(ln)).astype(BF16)
    y_b = jnp.dot(act, wco_ref[...], preferred_element_type=F32)

    m = sga_ref[...].astype(F32) * y_a + sgb_ref[...].astype(F32) * y_b
    mo = jnp.dot(m.astype(BF16), wo_ref[...], preferred_element_type=F32)
    o_ref[...] = x_ref[...] + _rms_norm(mo, gpm_ref[...])


def _mixer(z, a, sga, sgb, x2d, z_meta, a_meta, w_grp, pool_scale, w_pool_out,
           w_dw, b_dw, ln_g, ln_b, w_conv_out, w_o, g_post, *, batch, seq):
    tm = MIX_TM
    nt = seq // tm
    row = lambda b, t: (b * nt + t, 0)
    const2 = lambda b, t: (0, 0)
    const3 = lambda b, t: (0, 0, 0)
    once = pl.Buffered(1)
    return pl.pallas_call(
        _mixer_kernel,
        out_shape=jax.ShapeDtypeStruct((batch * seq, D_MODEL), F32),
        grid=(batch, nt),
        in_specs=[
            pl.BlockSpec((tm, POOL_WIDTH), row),
            pl.BlockSpec((tm, CONV_WIDTH), row),
            pl.BlockSpec((tm, D_MODEL), row),
            pl.BlockSpec((tm, D_MODEL), row),
            pl.BlockSpec((tm, D_MODEL), row),
            pl.BlockSpec((N_META, POOL_WIDTH), const2, pipeline_mode=once),
            pl.BlockSpec((N_META, CONV_WIDTH), const2, pipeline_mode=once),
            pl.BlockSpec((len(POOL_WINDOWS), POOL_GROUP_DIM, POOL_GROUP_DIM), const3,
                         pipeline_mode=once),
            pl.BlockSpec((1, POOL_WIDTH), const2, pipeline_mode=once),
            pl.BlockSpec((POOL_WIDTH, D_MODEL), const2, pipeline_mode=once),
            pl.BlockSpec((CONV_KERNEL, SUBLANES, CONV_WIDTH), const3, pipeline_mode=once),
            pl.BlockSpec((1, CONV_WIDTH), const2, pipeline_mode=once),
            pl.BlockSpec((1, CONV_WIDTH), const2, pipeline_mode=once),
            pl.BlockSpec((1, CONV_WIDTH), const2, pipeline_mode=once),
            pl.BlockSpec((CONV_WIDTH, D_MODEL), const2, pipeline_mode=once),
            pl.BlockSpec((D_MODEL, D_MODEL), const2, pipeline_mode=once),
            pl.BlockSpec((1, D_MODEL), const2, pipeline_mode=once),
        ],
        out_specs=pl.BlockSpec((tm, D_MODEL), row),
        scratch_shapes=[
            pltpu.VMEM((POOL_HALO + tm, POOL_WIDTH), F32),
            pltpu.VMEM((SUBLANES, CONV_HALO + tm, CONV_WIDTH), F32),
            pltpu.VMEM((tm, CONV_WIDTH), F32),
        ],
        compiler_params=pltpu.CompilerParams(
            dimension_semantics=("arbitrary", "arbitrary"),
            vmem_limit_bytes=V7X_VMEM_LIMIT_BYTES),
        name="mixer",
    )(z, a, sga, sgb, x2d, z_meta, a_meta, w_grp, pool_scale, w_pool_out,
      w_dw, b_dw, ln_g, ln_b, w_conv_out, w_o, g_post)


def _mlp_kernel(h_ref, g1_ref, wup_ref, wdn_ref, g2_ref, o_ref, u_scr):
    k = pl.program_id(1)
    last = pl.num_programs(1) - 1
    tm = h_ref.shape[0]
    mb = tm // MLP_ROW_BLOCKS
    row_blocks = [slice(r0, r0 + mb) for r0 in range(0, tm, mb)]

    def partial_out(rb):
        up = jnp.dot(u_scr[rb, :], wup_ref[...], preferred_element_type=F32)
        act = jnp.square(jnp.maximum(up, 0.0)).astype(BF16)
        return jnp.dot(act, wdn_ref[...], preferred_element_type=F32)

    def norm_in(rb):
        u_scr[rb, :] = _rms_norm(h_ref[rb, :], g1_ref[...]).astype(BF16)

    @pl.when(k == 0)
    def _():
        norm_in(row_blocks[0])
        for i, rb in enumerate(row_blocks):
            if i + 1 < len(row_blocks):
                norm_in(row_blocks[i + 1])
            o_ref[rb, :] = partial_out(rb)

    @pl.when(jnp.logical_and(k > 0, k < last))
    def _():
        o_ref[...] += partial_out(slice(None))

    @pl.when(k == last)
    def _():
        for rb in row_blocks:
            f = o_ref[rb, :] + partial_out(rb)
            o_ref[rb, :] = h_ref[rb, :] + _rms_norm(f, g2_ref[...])


def _mlp(h, g_pre, w_up, w_down, g_post):
    m = h.shape[0]
    row = lambda i, k: (i, 0)
    const2 = lambda i, k: (0, 0)
    return pl.pallas_call(
        _mlp_kernel,
        out_shape=jax.ShapeDtypeStruct((m, D_MODEL), F32),
        grid=(m // MLP_TM, D_FF // MLP_TF),
        in_specs=[
            pl.BlockSpec((MLP_TM, D_MODEL), row),
            pl.BlockSpec((1, D_MODEL), const2),
            pl.BlockSpec((D_MODEL, MLP_TF), lambda i, k: (0, k)),
            pl.BlockSpec((MLP_TF, D_MODEL), lambda i, k: (k, 0)),
            pl.BlockSpec((1, D_MODEL), const2),
        ],
        out_specs=pl.BlockSpec((MLP_TM, D_MODEL), row),
        scratch_shapes=[pltpu.VMEM((MLP_TM, D_MODEL), BF16)],
        compiler_params=pltpu.CompilerParams(
            dimension_semantics=("arbitrary", "arbitrary"),
            vmem_limit_bytes=V7X_VMEM_LIMIT_BYTES),
        name="mlp",
    )(h, g_pre, w_up, w_down, g_post)


def kernel(x, meta, g_pre_mix, w_in, w_pool_grp, pool_scale, w_pool_out, w_dw, b_dw, conv_ln_g, conv_ln_b, w_conv_out, w_o, g_post_mix, g_pre_mlp, w_up, w_down, g_post_mlp):
    batch, seq, _ = x.shape
    depth = w_in.shape[0]
    h = x.reshape(batch * seq, D_MODEL)
    meta_h = meta.astype(x.dtype)
    assert depth == 1, "meta rows are only carried as first-layer causal history"
    for l in range(depth):
        w_in_l = w_in[l].astype(BF16)
        g_pre = g_pre_mix[l][None]
        z_meta, a_meta = _meta_proj(meta_h, g_pre, w_in_l)
        (z, a, sga, sgb), (w_pool_out_b, w_conv_out_b, w_o_b, w_up_b, w_down_b) = _in_proj(
            h, g_pre, w_in_l, [w_pool_out[l], w_conv_out[l], w_o[l], w_up[l], w_down[l]])
        h = _mixer(z, a, sga, sgb, h, z_meta, a_meta,
                   w_pool_grp[l].astype(BF16), pool_scale[l][None], w_pool_out_b,
                   jnp.broadcast_to(w_dw[l][:, None, :], (CONV_KERNEL, SUBLANES, CONV_WIDTH)),
                   b_dw[l][None], conv_ln_g[l][None], conv_ln_b[l][None],
                   w_conv_out_b, w_o_b, g_post_mix[l][None], batch=batch, seq=seq)
        h = _mlp(h, g_pre_mlp[l][None], w_up_b, w_down_b, g_post_mlp[l][None])
    return h.reshape(batch, seq, D_MODEL)
```

```python
import jax
import jax.numpy as jnp
from jax import lax
from jax.experimental import pallas as pl
from jax.experimental.pallas import tpu as pltpu

D_MODEL = 2048
N_META = 16
POOL_WIDTH = D_MODEL // 2
POOL_WINDOWS = (2, 4, 8, 16)
POOL_GROUP_DIM = POOL_WIDTH // len(POOL_WINDOWS)
CONV_WIDTH = D_MODEL // 2
CONV_KERNEL = 31
D_FF = 4 * D_MODEL
IN_COLS = POOL_WIDTH + 2 * CONV_WIDTH + 2 * D_MODEL
RMS_EPS = 1e-6
LN_EPS = 1e-5

V7X_VMEM_LIMIT_BYTES = 56 * 1024 * 1024
SUBLANES = 8

F32 = jnp.float32
BF16 = jnp.bfloat16

IN_TM = 256
IN_CHUNK = 1024
COL_V = POOL_WIDTH
COL_G = COL_V + CONV_WIDTH
COL_GA = COL_G + CONV_WIDTH
COL_GB = COL_GA + D_MODEL

MIX_TM = 256
POOL_HALO = 16
CONV_HALO = 32
CONV_ROWS = 32
CONV_COLS = 256

MLP_TM = 512
MLP_TF = 2048
MLP_ROW_BLOCKS = 2


def _rms_norm(x, g):
    ms = jnp.mean(x * x, axis=-1, keepdims=True)
    return x * lax.rsqrt(ms + RMS_EPS) * g


def _sigmoid(x):
    return 0.5 * jnp.tanh(0.5 * x) + 0.5


def _meta_proj_kernel(x_ref, g_ref, w_ref, z_ref, a_ref):
    u = _rms_norm(x_ref[...], g_ref[...]).astype(BF16)

    def proj(c0):
        return jnp.dot(u, w_ref[:, c0:c0 + IN_CHUNK], preferred_element_type=F32)

    z_ref[...] = proj(0)
    a_ref[...] = proj(COL_V) * _sigmoid(proj(COL_G))


def _meta_proj(meta, g_pre, w_in):
    const = lambda i: (0, 0)
    return pl.pallas_call(
        _meta_proj_kernel,
        out_shape=[jax.ShapeDtypeStruct((N_META, POOL_WIDTH), F32),
                   jax.ShapeDtypeStruct((N_META, CONV_WIDTH), F32)],
        grid=(1,),
        in_specs=[
            pl.BlockSpec((N_META, D_MODEL), const),
            pl.BlockSpec((1, D_MODEL), const),
            pl.BlockSpec((D_MODEL, COL_GA), const),
        ],
        out_specs=[pl.BlockSpec((N_META, POOL_WIDTH), const),
                   pl.BlockSpec((N_META, CONV_WIDTH), const)],
        compiler_params=pltpu.CompilerParams(
            dimension_semantics=("arbitrary",),
            vmem_limit_bytes=V7X_VMEM_LIMIT_BYTES),
        name="meta_proj",
    )(meta, g_pre, w_in)


def _in_proj_kernel(x_ref, g_ref, w_ref, zm_ref, *refs):
    n_cast = (len(refs) - 5) // 2
    cast_in = refs[:n_cast]
    d_ref, a_ref, sga_ref, sgb_ref = refs[n_cast:n_cast + 4]
    cast_out = refs[n_cast + 4:-1]
    zs = refs[-1]
    tm = x_ref.shape[0]
    t = pl.program_id(1)

    @pl.when(t == 0)
    def _():
        zs[0:POOL_HALO, :] = zm_ref[...]

    @pl.when(t > 0)
    def _():
        zs[0:POOL_HALO, :] = zs[tm:tm + POOL_HALO, :]

    u = _rms_norm(x_ref[...], g_ref[...]).astype(BF16)

    def proj(c0):
        return jnp.dot(u, w_ref[:, c0:c0 + IN_CHUNK], preferred_element_type=F32)

    for c in range(0, POOL_WIDTH, IN_CHUNK):
        zs[POOL_HALO:POOL_HALO + tm, c:c + IN_CHUNK] = proj(c)

    for g, w in enumerate(POOL_WINDOWS):
        cols = slice(g * POOL_GROUP_DIM, (g + 1) * POOL_GROUP_DIM)
        zz = zs[:, cols]
        win = zz
        shift = 1
        while shift < w:
            win = win + pltpu.roll(win, shift, axis=0)
            shift *= 2
        d_ref[:, cols] = (win[POOL_HALO:] * (1.0 / w) - zz[POOL_HALO:]).astype(BF16)

    for c in range(0, CONV_WIDTH, IN_CHUNK):
        a_ref[:, c:c + IN_CHUNK] = proj(COL_V + c) * _sigmoid(proj(COL_G + c))
    for gate_ref, col0 in ((sga_ref, COL_GA), (sgb_ref, COL_GB)):
        for c in range(0, D_MODEL, IN_CHUNK):
            gate_ref[:, c:c + IN_CHUNK] = _sigmoid(proj(col0 + c)).astype(BF16)

    for src, dst in zip(cast_in, cast_out):
        dst[...] = src[...].astype(BF16)


def _in_proj(x2d, g_pre, w_in, z_meta, later_weights, *, batch, seq):
    tm = IN_TM
    nt = seq // tm
    m = batch * seq
    n_steps = batch * nt
    row = lambda b, t: (b * nt + t, 0)
    const = lambda b, t: (0, 0)
    once = pl.Buffered(1)
    cast_specs = [pl.BlockSpec((w.shape[0] // n_steps, w.shape[1]), row) for w in later_weights]
    outs = pl.pallas_call(
        _in_proj_kernel,
        out_shape=[jax.ShapeDtypeStruct((m, POOL_WIDTH), BF16),
                   jax.ShapeDtypeStruct((m, CONV_WIDTH), F32),
                   jax.ShapeDtypeStruct((m, D_MODEL), BF16),
                   jax.ShapeDtypeStruct((m, D_MODEL), BF16)]
                  + [jax.ShapeDtypeStruct(w.shape, BF16) for w in later_weights],
        grid=(batch, nt),
        in_specs=[
            pl.BlockSpec((tm, D_MODEL), row),
            pl.BlockSpec((1, D_MODEL), const, pipeline_mode=once),
            pl.BlockSpec((D_MODEL, IN_COLS), const, pipeline_mode=once),
            pl.BlockSpec((N_META, POOL_WIDTH), const, pipeline_mode=once),
        ] + cast_specs,
        out_specs=[pl.BlockSpec((tm, POOL_WIDTH), row), pl.BlockSpec((tm, CONV_WIDTH), row),
                   pl.BlockSpec((tm, D_MODEL), row), pl.BlockSpec((tm, D_MODEL), row)] + cast_specs,
        scratch_shapes=[pltpu.VMEM((POOL_HALO + tm, POOL_WIDTH), F32)],
        compiler_params=pltpu.CompilerParams(
            dimension_semantics=("arbitrary", "arbitrary"),
            vmem_limit_bytes=V7X_VMEM_LIMIT_BYTES),
        name="in_proj",
    )(x2d, g_pre, w_in, z_meta, *later_weights)
    return outs[:4], outs[4:]


def _mixer_kernel(d_ref, a_ref, sga_ref, sgb_ref, x_ref, am_ref,
                  wgrp_ref, pscale_ref, wpo_ref, wdw_ref, bdw_ref,
                  lng_ref, lnb_ref, wco_ref, wo_ref, gpm_ref,
                  o_ref, cs, conv_scr):
    tm = a_ref.shape[0]
    t = pl.program_id(1)

    @pl.when(t == 0)
    def _():
        cs[0, 0:CONV_HALO - N_META, :] = jnp.zeros((CONV_HALO - N_META, CONV_WIDTH), F32)
        cs[0, CONV_HALO - N_META:CONV_HALO, :] = am_ref[...]

    @pl.when(t > 0)
    def _():
        cs[0, 0:CONV_HALO, :] = cs[0, tm:tm + CONV_HALO, :]

    cs[0, CONV_HALO:CONV_HALO + tm, :] = a_ref[...]

    n_rows = CONV_HALO + tm
    conv_in = cs[0]
    for s in range(1, SUBLANES):
        cs[s] = pltpu.roll(conv_in, n_rows - s, axis=0)

    ya_parts = []
    for g in range(len(POOL_WINDOWS)):
        cols = slice(g * POOL_GROUP_DIM, (g + 1) * POOL_GROUP_DIM)
        y = jnp.dot(d_ref[:, cols], wgrp_ref[g], preferred_element_type=F32)
        ya_parts.append((y * pscale_ref[:, cols]).astype(BF16))
    pooled = jnp.concatenate(ya_parts, axis=-1)
    y_a = jnp.dot(pooled, wpo_ref[...], preferred_element_type=F32)

    for cb in range(CONV_WIDTH // CONV_COLS):
        c0 = cb * CONV_COLS
        c1 = c0 + CONV_COLS
        for r0 in range(0, tm, CONV_ROWS):
            acc = jnp.broadcast_to(bdw_ref[:, c0:c1], (CONV_ROWS, CONV_COLS))
            for k in range(CONV_KERNEL):
                q, s = divmod(r0 + CONV_HALO - (CONV_KERNEL - 1) + k, SUBLANES)
                rows = slice(q * SUBLANES, q * SUBLANES + CONV_ROWS)
                wk = wdw_ref[k, :, c0:c1]
                acc = acc + cs[s, rows, c0:c1] * jnp.concatenate(
                    [wk] * (CONV_ROWS // SUBLANES), axis=0)
            conv_scr[r0:r0 + CONV_ROWS, c0:c1] = acc

    c = conv_scr[...]
    mu = jnp.mean(c, axis=-1, keepdims=True)
    cc = c - mu
    var = jnp.mean(cc * cc, axis=-1, keepdims=True)
    ln = cc * lax.rsqrt(var + LN_EPS) * lng_ref[...] + lnb_ref[...]
    act = (ln * _sigmoid(ln)).astype(BF16)
    y_b = jnp.dot(act, wco_ref[...], preferred_element_type=F32)

    m = sga_ref[...].astype(F32) * y_a + sgb_ref[...].astype(F32) * y_b
    mo = jnp.dot(m.astype(BF16), wo_ref[...], preferred_element_type=F32)
    o_ref[...] = x_ref[...] + _rms_norm(mo, gpm_ref[...])


def _mixer(d, a, sga, sgb, x2d, a_meta, w_grp, pool_scale, w_pool_out,
           w_dw, b_dw, ln_g, ln_b, w_conv_out, w_o, g_post, *, batch, seq):
    tm = MIX_TM
    nt = seq // tm
    row = lambda b, t: (b * nt + t, 0)
    const2 = lambda b, t: (0, 0)
    const3 = lambda b, t: (0, 0, 0)
    once = pl.Buffered(1)
    return pl.pallas_call(
        _mixer_kernel,
        out_shape=jax.ShapeDtypeStruct((batch * seq, D_MODEL), F32),
        grid=(batch, nt),
        in_specs=[
            pl.BlockSpec((tm, POOL_WIDTH), row),
            pl.BlockSpec((tm, CONV_WIDTH), row),
            pl.BlockSpec((tm, D_MODEL), row),
            pl.BlockSpec((tm, D_MODEL), row),
            pl.BlockSpec((tm, D_MODEL), row),
            pl.BlockSpec((N_META, CONV_WIDTH), const2, pipeline_mode=once),
            pl.BlockSpec((len(POOL_WINDOWS), POOL_GROUP_DIM, POOL_GROUP_DIM), const3,
                         pipeline_mode=once),
            pl.BlockSpec((1, POOL_WIDTH), const2, pipeline_mode=once),
            pl.BlockSpec((POOL_WIDTH, D_MODEL), const2, pipeline_mode=once),
            pl.BlockSpec((CONV_KERNEL, SUBLANES, CONV_WIDTH), const3, pipeline_mode=once),
            pl.BlockSpec((1, CONV_WIDTH), const2, pipeline_mode=once),
            pl.BlockSpec((1, CONV_WIDTH), const2, pipeline_mode=once),
            pl.BlockSpec((1, CONV_WIDTH), const2, pipeline_mode=once),
            pl.BlockSpec((CONV_WIDTH, D_MODEL), const2, pipeline_mode=once),
            pl.BlockSpec((D_MODEL, D_MODEL), const2, pipeline_mode=once),
            pl.BlockSpec((1, D_MODEL), const2, pipeline_mode=once),
        ],
        out_specs=pl.BlockSpec((tm, D_MODEL), row),
        scratch_shapes=[
            pltpu.VMEM((SUBLANES, CONV_HALO + tm, CONV_WIDTH), F32),
            pltpu.VMEM((tm, CONV_WIDTH), F32),
        ],
        compiler_params=pltpu.CompilerParams(
            dimension_semantics=("arbitrary", "arbitrary"),
            vmem_limit_bytes=V7X_VMEM_LIMIT_BYTES),
        name="mixer",
    )(d, a, sga, sgb, x2d, a_meta, w_grp, pool_scale, w_pool_out,
      w_dw, b_dw, ln_g, ln_b, w_conv_out, w_o, g_post)


def _mlp_kernel(h_ref, g1_ref, wup_ref, wdn_ref, g2_ref, o_ref, u_scr):
    k = pl.program_id(1)
    last = pl.num_programs(1) - 1
    tm = h_ref.shape[0]
    mb = tm // MLP_ROW_BLOCKS
    row_blocks = [slice(r0, r0 + mb) for r0 in range(0, tm, mb)]

    def partial_out(rb):
        up = jnp.dot(u_scr[rb, :], wup_ref[...], preferred_element_type=F32)
        act = jnp.square(jnp.maximum(up, 0.0)).astype(BF16)
        return jnp.dot(act, wdn_ref[...], preferred_element_type=F32)

    def norm_in(rb):
        u_scr[rb, :] = _rms_norm(h_ref[rb, :], g1_ref[...]).astype(BF16)

    @pl.when(k == 0)
    def _():
        norm_in(row_blocks[0])
        for i, rb in enumerate(row_blocks):
            if i + 1 < len(row_blocks):
                norm_in(row_blocks[i + 1])
            o_ref[rb, :] = partial_out(rb)

    @pl.when(jnp.logical_and(k > 0, k < last))
    def _():
        o_ref[...] += partial_out(slice(None))

    @pl.when(k == last)
    def _():
        for rb in row_blocks:
            f = o_ref[rb, :] + partial_out(rb)
            o_ref[rb, :] = h_ref[rb, :] + _rms_norm(f, g2_ref[...])


def _mlp(h, g_pre, w_up, w_down, g_post):
    m = h.shape[0]
    row = lambda i, k: (i, 0)
    const2 = lambda i, k: (0, 0)
    return pl.pallas_call(
        _mlp_kernel,
        out_shape=jax.ShapeDtypeStruct((m, D_MODEL), F32),
        grid=(m // MLP_TM, D_FF // MLP_TF),
        in_specs=[
            pl.BlockSpec((MLP_TM, D_MODEL), row),
            pl.BlockSpec((1, D_MODEL), const2),
            pl.BlockSpec((D_MODEL, MLP_TF), lambda i, k: (0, k)),
            pl.BlockSpec((MLP_TF, D_MODEL), lambda i, k: (k, 0)),
            pl.BlockSpec((1, D_MODEL), const2),
        ],
        out_specs=pl.BlockSpec((MLP_TM, D_MODEL), row),
        scratch_shapes=[pltpu.VMEM((MLP_TM, D_MODEL), BF16)],
        compiler_params=pltpu.CompilerParams(
            dimension_semantics=("arbitrary", "arbitrary"),
            vmem_limit_bytes=V7X_VMEM_LIMIT_BYTES),
        name="mlp",
    )(h, g_pre, w_up, w_down, g_post)


def kernel(x, meta, g_pre_mix, w_in, w_pool_grp, pool_scale, w_pool_out, w_dw, b_dw, conv_ln_g, conv_ln_b, w_conv_out, w_o, g_post_mix, g_pre_mlp, w_up, w_down, g_post_mlp):
    batch, seq, _ = x.shape
    depth = w_in.shape[0]
    h = x.reshape(batch * seq, D_MODEL)
    meta_h = meta.astype(x.dtype)
    assert depth == 1, "meta rows are only carried as first-layer causal history"
    for l in range(depth):
        w_in_l = w_in[l].astype(BF16)
        g_pre = g_pre_mix[l][None]
        z_meta, a_meta = _meta_proj(meta_h, g_pre, w_in_l)
        (d, a, sga, sgb), (w_pool_out_b, w_conv_out_b, w_o_b, w_up_b, w_down_b) = _in_proj(
            h, g_pre, w_in_l, z_meta,
            [w_pool_out[l], w_conv_out[l], w_o[l], w_up[l], w_down[l]], batch=batch, seq=seq)
        h = _mixer(d, a, sga, sgb, h, a_meta,
                   w_pool_grp[l].astype(BF16), pool_scale[l][None], w_pool_out_b,
                   jnp.broadcast_to(w_dw[l][:, None, :], (CONV_KERNEL, SUBLANES, CONV_WIDTH)),
                   b_dw[l][None], conv_ln_g[l][None], conv_ln_b[l][None],
                   w_conv_out_b, w_o_b, g_post_mix[l][None], batch=batch, seq=seq)
        h = _mlp(h, g_pre_mlp[l][None], w_up_b, w_down_b, g_post_mlp[l][None])
    return h.reshape(batch, seq, D_MODEL)
```

```python
import jax
import jax.numpy as jnp
from jax import lax
from jax.experimental import pallas as pl
from jax.experimental.pallas import tpu as pltpu

D_MODEL = 2048
N_META = 16
POOL_WIDTH = D_MODEL // 2
POOL_WINDOWS = (2, 4, 8, 16)
POOL_GROUP_DIM = POOL_WIDTH // len(POOL_WINDOWS)
CONV_WIDTH = D_MODEL // 2
CONV_KERNEL = 31
D_FF = 4 * D_MODEL
IN_COLS = POOL_WIDTH + 2 * CONV_WIDTH + 2 * D_MODEL
RMS_EPS = 1e-6
LN_EPS = 1e-5

V7X_VMEM_LIMIT_BYTES = 56 * 1024 * 1024
SUBLANES = 8

F32 = jnp.float32
BF16 = jnp.bfloat16

IN_TM = 256
IN_CHUNK = 1024
COL_V = POOL_WIDTH
COL_G = COL_V + CONV_WIDTH
COL_GA = COL_G + CONV_WIDTH
COL_GB = COL_GA + D_MODEL

MIX_TM = 256
POOL_HALO = 16
CONV_HALO = 32
CONV_ROWS = 32
CONV_COLS = 256

MLP_TM = 512
MLP_TF = 2048
MLP_ROW_BLOCKS = 2


def _rms_norm(x, g):
    ms = jnp.mean(x * x, axis=-1, keepdims=True)
    return x * lax.rsqrt(ms + RMS_EPS) * g


def _sigmoid(x):
    return 0.5 * jnp.tanh(0.5 * x) + 0.5


def _meta_proj_kernel(x_ref, g_ref, w_ref, z_ref, a_ref):
    u = _rms_norm(x_ref[...], g_ref[...]).astype(BF16)

    def proj(c0):
        return jnp.dot(u, w_ref[:, c0:c0 + IN_CHUNK], preferred_element_type=F32)

    z_ref[...] = proj(0)
    a_ref[...] = proj(COL_V) * _sigmoid(proj(COL_G))


def _meta_proj(meta, g_pre, w_in):
    const = lambda i: (0, 0)
    return pl.pallas_call(
        _meta_proj_kernel,
        out_shape=[jax.ShapeDtypeStruct((N_META, POOL_WIDTH), F32),
                   jax.ShapeDtypeStruct((N_META, CONV_WIDTH), F32)],
        grid=(1,),
        in_specs=[
            pl.BlockSpec((N_META, D_MODEL), const),
            pl.BlockSpec((1, D_MODEL), const),
            pl.BlockSpec((D_MODEL, COL_GA), const),
        ],
        out_specs=[pl.BlockSpec((N_META, POOL_WIDTH), const),
                   pl.BlockSpec((N_META, CONV_WIDTH), const)],
        compiler_params=pltpu.CompilerParams(
            dimension_semantics=("arbitrary",),
            vmem_limit_bytes=V7X_VMEM_LIMIT_BYTES),
        name="meta_proj",
    )(meta, g_pre, w_in)


def _in_proj_kernel(x_ref, g_ref, w_ref, *refs):
    n_cast = (len(refs) - 4) // 2
    cast_in = refs[:n_cast]
    z_ref, a_ref, sga_ref, sgb_ref = refs[n_cast:n_cast + 4]
    cast_out = refs[n_cast + 4:]

    u = _rms_norm(x_ref[...], g_ref[...]).astype(BF16)

    def proj(c0):
        return jnp.dot(u, w_ref[:, c0:c0 + IN_CHUNK], preferred_element_type=F32)

    for c in range(0, POOL_WIDTH, IN_CHUNK):
        z_ref[:, c:c + IN_CHUNK] = proj(c)
    for c in range(0, CONV_WIDTH, IN_CHUNK):
        a_ref[:, c:c + IN_CHUNK] = proj(COL_V + c) * _sigmoid(proj(COL_G + c))
    for gate_ref, col0 in ((sga_ref, COL_GA), (sgb_ref, COL_GB)):
        for c in range(0, D_MODEL, IN_CHUNK):
            gate_ref[:, c:c + IN_CHUNK] = _sigmoid(proj(col0 + c)).astype(BF16)

    for src, dst in zip(cast_in, cast_out):
        dst[...] = src[...].astype(BF16)


def _in_proj(x2d, g_pre, w_in, later_weights):
    tm = IN_TM
    m = x2d.shape[0]
    n_steps = m // tm
    row = lambda i: (i, 0)
    const = lambda i: (0, 0)
    once = pl.Buffered(1)
    cast_specs = [pl.BlockSpec((w.shape[0] // n_steps, w.shape[1]), row) for w in later_weights]
    outs = pl.pallas_call(
        _in_proj_kernel,
        out_shape=[jax.ShapeDtypeStruct((m, POOL_WIDTH), F32),
                   jax.ShapeDtypeStruct((m, CONV_WIDTH), F32),
                   jax.ShapeDtypeStruct((m, D_MODEL), BF16),
                   jax.ShapeDtypeStruct((m, D_MODEL), BF16)]
                  + [jax.ShapeDtypeStruct(w.shape, BF16) for w in later_weights],
        grid=(n_steps,),
        in_specs=[
            pl.BlockSpec((tm, D_MODEL), row),
            pl.BlockSpec((1, D_MODEL), const, pipeline_mode=once),
            pl.BlockSpec((D_MODEL, IN_COLS), const, pipeline_mode=once),
        ] + cast_specs,
        out_specs=[pl.BlockSpec((tm, POOL_WIDTH), row), pl.BlockSpec((tm, CONV_WIDTH), row),
                   pl.BlockSpec((tm, D_MODEL), row), pl.BlockSpec((tm, D_MODEL), row)] + cast_specs,
        compiler_params=pltpu.CompilerParams(
            dimension_semantics=("arbitrary",),
            vmem_limit_bytes=V7X_VMEM_LIMIT_BYTES),
        name="in_proj",
    )(x2d, g_pre, w_in, *later_weights)
    return outs[:4], outs[4:]


def _mixer_kernel(z_ref, a_ref, sga_ref, sgb_ref, x_ref, zm_ref, am_ref,
                  wgrp_ref, pscale_ref, wpo_ref, wdw_ref, bdw_ref,
                  lng_ref, lnb_ref, wco_ref, wo_ref, gpm_ref,
                  o_ref, zs, cs, conv_scr):
    tm = z_ref.shape[0]
    t = pl.program_id(1)

    @pl.when(t == 0)
    def _():
        zs[0:POOL_HALO, :] = zm_ref[...]
        cs[0, 0:CONV_HALO - N_META, :] = jnp.zeros((CONV_HALO - N_META, CONV_WIDTH), F32)
        cs[0, CONV_HALO - N_META:CONV_HALO, :] = am_ref[...]

    @pl.when(t > 0)
    def _():
        zs[0:POOL_HALO, :] = zs[tm:tm + POOL_HALO, :]
        cs[0, 0:CONV_HALO, :] = cs[0, tm:tm + CONV_HALO, :]

    zs[POOL_HALO:POOL_HALO + tm, :] = z_ref[...]
    cs[0, CONV_HALO:CONV_HALO + tm, :] = a_ref[...]

    n_rows = CONV_HALO + tm
    conv_in = cs[0]
    for s in range(1, SUBLANES):
        cs[s] = pltpu.roll(conv_in, n_rows - s, axis=0)

    ya_parts = []
    for g, w in enumerate(POOL_WINDOWS):
        cols = slice(g * POOL_GROUP_DIM, (g + 1) * POOL_GROUP_DIM)
        zz = zs[:, cols]
        win = zz
        shift = 1
        while shift < w:
            win = win + pltpu.roll(win, shift, axis=0)
            shift *= 2
        d = (win[POOL_HALO:] * (1.0 / w) - zz[POOL_HALO:]).astype(BF16)
        y = jnp.dot(d, wgrp_ref[g], preferred_element_type=F32)
        ya_parts.append((y * pscale_ref[:, cols]).astype(BF16))
    pooled = jnp.concatenate(ya_parts, axis=-1)
    y_a = jnp.dot(pooled, wpo_ref[...], preferred_element_type=F32)

    for cb in range(CONV_WIDTH // CONV_COLS):
        c0 = cb * CONV_COLS
        c1 = c0 + CONV_COLS
        for r0 in range(0, tm, CONV_ROWS):
            acc = jnp.broadcast_to(bdw_ref[:, c0:c1], (CONV_ROWS, CONV_COLS))
            for k in range(CONV_KERNEL):
                q, s = divmod(r0 + CONV_HALO - (CONV_KERNEL - 1) + k, SUBLANES)
                rows = slice(q * SUBLANES, q * SUBLANES + CONV_ROWS)
                wk = wdw_ref[k, :, c0:c1]
                acc = acc + cs[s, rows, c0:c1] * jnp.concatenate(
                    [wk] * (CONV_ROWS // SUBLANES), axis=0)
            conv_scr[r0:r0 + CONV_ROWS, c0:c1] = acc

    c = conv_scr[...]
    mu = jnp.mean(c, axis=-1, keepdims=True)
    cc = c - mu
    var = jnp.mean(cc * cc, axis=-1, keepdims=True)
    ln = cc * lax.rsqrt(var + LN_EPS) * lng_ref[...] + lnb_ref[...]
    act = (ln * _sigmoid(ln)).astype(BF16)
    y_b = jnp.dot(act, wco_ref[...], preferred_element_type=F32)

    m = sga_ref[...].astype(F32) * y_a + sgb_ref[...].astype(F32) * y_b
    mo = jnp.dot(m.astype(BF16), wo_ref[...], preferred_element_type=F32)
    o_ref[...] = x_ref[...] + _rms_norm(mo, gpm_ref[...])


def _mixer(z, a, sga, sgb, x2d, z_meta, a_meta, w_grp, pool_scale, w_pool_out,
           w_dw, b_dw, ln_g, ln_b, w_conv_out, w_o, g_post, *, batch, seq):
    tm = MIX_TM
    nt = seq // tm
    row = lambda b, t: (b * nt + t, 0)
    const2 = lambda b, t: (0, 0)
    const3 = lambda b, t: (0, 0, 0)
    once = pl.Buffered(1)
    return pl.pallas_call(
        _mixer_kernel,
        out_shape=jax.ShapeDtypeStruct((batch * seq, D_MODEL), F32),
        grid=(batch, nt),
        in_specs=[
            pl.BlockSpec((tm, POOL_WIDTH), row),
            pl.BlockSpec((tm, CONV_WIDTH), row),
            pl.BlockSpec((tm, D_MODEL), row),
            pl.BlockSpec((tm, D_MODEL), row),
            pl.BlockSpec((tm, D_MODEL), row),
            pl.BlockSpec((N_META, POOL_WIDTH), const2, pipeline_mode=once),
            pl.BlockSpec((N_META, CONV_WIDTH), const2, pipeline_mode=once),
            pl.BlockSpec((len(POOL_WINDOWS), POOL_GROUP_DIM, POOL_GROUP_DIM), const3,
                         pipeline_mode=once),
            pl.BlockSpec((1, POOL_WIDTH), const2, pipeline_mode=once),
            pl.BlockSpec((POOL_WIDTH, D_MODEL), const2, pipeline_mode=once),
            pl.BlockSpec((CONV_KERNEL, SUBLANES, CONV_WIDTH), const3, pipeline_mode=once),
            pl.BlockSpec((1, CONV_WIDTH), const2, pipeline_mode=once),
            pl.BlockSpec((1, CONV_WIDTH), const2, pipeline_mode=once),
            pl.BlockSpec((1, CONV_WIDTH), const2, pipeline_mode=once),
            pl.BlockSpec((CONV_WIDTH, D_MODEL), const2, pipeline_mode=once),
            pl.BlockSpec((D_MODEL, D_MODEL), const2, pipeline_mode=once),
            pl.BlockSpec((1, D_MODEL), const2, pipeline_mode=once),
        ],
        out_specs=pl.BlockSpec((tm, D_MODEL), row),
        scratch_shapes=[
            pltpu.VMEM((POOL_HALO + tm, POOL_WIDTH), F32),
            pltpu.VMEM((SUBLANES, CONV_HALO + tm, CONV_WIDTH), F32),
            pltpu.VMEM((tm, CONV_WIDTH), F32),
        ],
        compiler_params=pltpu.CompilerParams(
            dimension_semantics=("arbitrary", "arbitrary"),
            vmem_limit_bytes=V7X_VMEM_LIMIT_BYTES),
        name="mixer",
    )(z, a, sga, sgb, x2d, z_meta, a_meta, w_grp, pool_scale, w_pool_out,
      w_dw, b_dw, ln_g, ln_b, w_conv_out, w_o, g_post)


def _mlp_step(h_ref, wup_ref, wdn_ref, o_ref, g1_ref, g2_ref, u_scr):
    k = pl.program_id(1)
    last = D_FF // MLP_TF - 1
    tm = h_ref.shape[0]
    mb = tm // MLP_ROW_BLOCKS
    row_blocks = [slice(r0, r0 + mb) for r0 in range(0, tm, mb)]

    def partial_out(rb):
        up = jnp.dot(u_scr[rb, :], wup_ref[...], preferred_element_type=F32)
        act = jnp.square(jnp.maximum(up, 0.0)).astype(BF16)
        return jnp.dot(act, wdn_ref[...], preferred_element_type=F32)

    def norm_in(rb):
        u_scr[rb, :] = _rms_norm(h_ref[rb, :], g1_ref[...]).astype(BF16)

    @pl.when(k == 0)
    def _():
        norm_in(row_blocks[0])
        for i, rb in enumerate(row_blocks):
            if i + 1 < len(row_blocks):
                norm_in(row_blocks[i + 1])
            o_ref[rb, :] = partial_out(rb)

    @pl.when(jnp.logical_and(k > 0, k < last))
    def _():
        o_ref[...] += partial_out(slice(None))

    @pl.when(k == last)
    def _():
        for rb in row_blocks:
            f = o_ref[rb, :] + partial_out(rb)
            o_ref[rb, :] = h_ref[rb, :] + _rms_norm(f, g2_ref[...])


def _mlp_kernel(h_hbm, g1_ref, wup_hbm, wdn_hbm, g2_ref, o_hbm, u_scr):
    m = h_hbm.shape[0]
    row = lambda i, k: (i, 0)
    pltpu.emit_pipeline(
        _mlp_step,
        grid=(m // MLP_TM, D_FF // MLP_TF),
        in_specs=[
            pl.BlockSpec((MLP_TM, D_MODEL), row,
                         pipeline_mode=pl.Buffered(2, use_lookahead=True)),
            pl.BlockSpec((D_MODEL, MLP_TF), lambda i, k: (0, k)),
            pl.BlockSpec((MLP_TF, D_MODEL), lambda i, k: (k, 0)),
        ],
        out_specs=[pl.BlockSpec((MLP_TM, D_MODEL), row)],
    )(h_hbm, wup_hbm, wdn_hbm, o_hbm, scratches=(g1_ref, g2_ref, u_scr))


def _mlp(h, g_pre, w_up, w_down, g_post):
    m = h.shape[0]
    hbm = pl.BlockSpec(memory_space=pl.ANY)
    vmem = pl.BlockSpec(memory_space=pltpu.VMEM)
    return pl.pallas_call(
        _mlp_kernel,
        out_shape=jax.ShapeDtypeStruct((m, D_MODEL), F32),
        in_specs=[hbm, vmem, hbm, hbm, vmem],
        out_specs=hbm,
        scratch_shapes=[pltpu.VMEM((MLP_TM, D_MODEL), BF16)],
        compiler_params=pltpu.CompilerParams(vmem_limit_bytes=V7X_VMEM_LIMIT_BYTES),
        name="mlp",
    )(h, g_pre, w_up, w_down, g_post)


def kernel(x, meta, g_pre_mix, w_in, w_pool_grp, pool_scale, w_pool_out, w_dw, b_dw, conv_ln_g, conv_ln_b, w_conv_out, w_o, g_post_mix, g_pre_mlp, w_up, w_down, g_post_mlp):
    batch, seq, _ = x.shape
    depth = w_in.shape[0]
    h = x.reshape(batch * seq, D_MODEL)
    meta_h = meta.astype(x.dtype)
    assert depth == 1, "meta rows are only carried as first-layer causal history"
    for l in range(depth):
        w_in_l = w_in[l].astype(BF16)
        g_pre = g_pre_mix[l][None]
        z_meta, a_meta = _meta_proj(meta_h, g_pre, w_in_l)
        (z, a, sga, sgb), (w_pool_out_b, w_conv_out_b, w_o_b, w_up_b, w_down_b) = _in_proj(
            h, g_pre, w_in_l, [w_pool_out[l], w_conv_out[l], w_o[l], w_up[l], w_down[l]])
        h = _mixer(z, a, sga, sgb, h, z_meta, a_meta,
                   w_pool_grp[l].astype(BF16), pool_scale[l][None], w_pool_out_b,
                   jnp.broadcast_to(w_dw[l][:, None, :], (CONV_KERNEL, SUBLANES, CONV_WIDTH)),
                   b_dw[l][None], conv_ln_g[l][None], conv_ln_b[l][None],
                   w_conv_out_b, w_o_b, g_post_mix[l][None], batch=batch, seq=seq)
        h = _mlp(h, g_pre_mlp[l][None], w_up_b, w_down_b, g_post_mlp[l][None])
    return h.reshape(batch, seq, D_MODEL)
```

```python
import jax
import jax.numpy as jnp
from jax import lax
from jax.experimental import pallas as pl
from jax.experimental.pallas import tpu as pltpu

D_MODEL = 2048
N_META = 16
POOL_WIDTH = D_MODEL // 2
POOL_WINDOWS = (2, 4, 8, 16)
POOL_GROUP_DIM = POOL_WIDTH // len(POOL_WINDOWS)
CONV_WIDTH = D_MODEL // 2
CONV_KERNEL = 31
D_FF = 4 * D_MODEL
IN_COLS = POOL_WIDTH + 2 * CONV_WIDTH + 2 * D_MODEL
RMS_EPS = 1e-6
LN_EPS = 1e-5

V7X_VMEM_LIMIT_BYTES = 56 * 1024 * 1024
SUBLANES = 8
LANES = 128

F32 = jnp.float32
BF16 = jnp.bfloat16

IN_TM = 256
IN_CHUNK = 1024
COL_V = POOL_WIDTH
COL_G = COL_V + CONV_WIDTH
COL_GA = COL_G + CONV_WIDTH
COL_GB = COL_GA + D_MODEL

MIX_TM = 256
POOL_HALO = 16
CONV_HALO = 32
CONV_STEPS = 8

MLP_TM = 512
MLP_TF = 2048
MLP_ROW_BLOCKS = 2


def _rms_norm(x, g):
    ms = jnp.mean(x * x, axis=-1, keepdims=True)
    return x * lax.rsqrt(ms + RMS_EPS) * g


def _sigmoid(x):
    return 0.5 * jnp.tanh(0.5 * x) + 0.5


def _meta_proj_kernel(x_ref, g_ref, w_ref, z_ref, a_ref):
    u = _rms_norm(x_ref[...], g_ref[...]).astype(BF16)

    def proj(c0):
        return jnp.dot(u, w_ref[:, c0:c0 + IN_CHUNK], preferred_element_type=F32)

    z_ref[...] = proj(0)
    a_ref[...] = proj(COL_V) * _sigmoid(proj(COL_G))


def _meta_proj(meta, g_pre, w_in):
    const = lambda i: (0, 0)
    return pl.pallas_call(
        _meta_proj_kernel,
        out_shape=[jax.ShapeDtypeStruct((N_META, POOL_WIDTH), F32),
                   jax.ShapeDtypeStruct((N_META, CONV_WIDTH), F32)],
        grid=(1,),
        in_specs=[
            pl.BlockSpec((N_META, D_MODEL), const),
            pl.BlockSpec((1, D_MODEL), const),
            pl.BlockSpec((D_MODEL, COL_GA), const),
        ],
        out_specs=[pl.BlockSpec((N_META, POOL_WIDTH), const),
                   pl.BlockSpec((N_META, CONV_WIDTH), const)],
        compiler_params=pltpu.CompilerParams(
            dimension_semantics=("arbitrary",),
            vmem_limit_bytes=V7X_VMEM_LIMIT_BYTES),
        name="meta_proj",
    )(meta, g_pre, w_in)


def _in_proj_kernel(x_ref, g_ref, w_ref, *refs):
    n_cast = (len(refs) - 4) // 2
    cast_in = refs[:n_cast]
    z_ref, a_ref, sga_ref, sgb_ref = refs[n_cast:n_cast + 4]
    cast_out = refs[n_cast + 4:]

    u = _rms_norm(x_ref[...], g_ref[...]).astype(BF16)

    def proj(c0):
        return jnp.dot(u, w_ref[:, c0:c0 + IN_CHUNK], preferred_element_type=F32)

    for c in range(0, POOL_WIDTH, IN_CHUNK):
        z_ref[:, c:c + IN_CHUNK] = proj(c)
    for c in range(0, CONV_WIDTH, IN_CHUNK):
        a_ref[:, c:c + IN_CHUNK] = proj(COL_V + c) * _sigmoid(proj(COL_G + c))
    for gate_ref, col0 in ((sga_ref, COL_GA), (sgb_ref, COL_GB)):
        for c in range(0, D_MODEL, IN_CHUNK):
            gate_ref[:, c:c + IN_CHUNK] = _sigmoid(proj(col0 + c)).astype(BF16)

    for src, dst in zip(cast_in, cast_out):
        dst[...] = src[...].astype(BF16)


def _in_proj(x2d, g_pre, w_in, later_weights):
    tm = IN_TM
    m = x2d.shape[0]
    n_steps = m // tm
    row = lambda i: (i, 0)
    const = lambda i: (0, 0)
    once = pl.Buffered(1)
    cast_specs = [pl.BlockSpec((w.shape[0] // n_steps, w.shape[1]), row) for w in later_weights]
    outs = pl.pallas_call(
        _in_proj_kernel,
        out_shape=[jax.ShapeDtypeStruct((m, POOL_WIDTH), F32),
                   jax.ShapeDtypeStruct((m, CONV_WIDTH), F32),
                   jax.ShapeDtypeStruct((m, D_MODEL), BF16),
                   jax.ShapeDtypeStruct((m, D_MODEL), BF16)]
                  + [jax.ShapeDtypeStruct(w.shape, BF16) for w in later_weights],
        grid=(n_steps,),
        in_specs=[
            pl.BlockSpec((tm, D_MODEL), row),
            pl.BlockSpec((1, D_MODEL), const, pipeline_mode=once),
            pl.BlockSpec((D_MODEL, IN_COLS), const, pipeline_mode=once),
        ] + cast_specs,
        out_specs=[pl.BlockSpec((tm, POOL_WIDTH), row), pl.BlockSpec((tm, CONV_WIDTH), row),
                   pl.BlockSpec((tm, D_MODEL), row), pl.BlockSpec((tm, D_MODEL), row)] + cast_specs,
        compiler_params=pltpu.CompilerParams(
            dimension_semantics=("arbitrary",),
            vmem_limit_bytes=V7X_VMEM_LIMIT_BYTES),
        name="in_proj",
    )(x2d, g_pre, w_in, *later_weights)
    return outs[:4], outs[4:]


def _mixer_kernel(z_ref, a_ref, sga_ref, sgb_ref, x_ref, zm_ref, am_ref,
                  wgrp_ref, pscale_ref, wpo_ref, wdw_ref, bdw_ref,
                  lng_ref, lnb_ref, wco_ref, wo_ref, gpm_ref,
                  o_ref, zs, cs, conv_scr):
    tm = z_ref.shape[0]
    t = pl.program_id(1)

    @pl.when(t == 0)
    def _():
        zs[0:POOL_HALO, :] = zm_ref[...]
        cs[0:CONV_HALO - N_META] = jnp.zeros((CONV_HALO - N_META, SUBLANES, LANES), F32)
        cs[CONV_HALO - N_META:CONV_HALO] = am_ref[...].reshape(N_META, SUBLANES, LANES)

    @pl.when(t > 0)
    def _():
        zs[0:POOL_HALO, :] = zs[tm:tm + POOL_HALO, :]
        cs[0:CONV_HALO] = cs[tm:tm + CONV_HALO]

    zs[POOL_HALO:POOL_HALO + tm, :] = z_ref[...]
    cs[CONV_HALO:CONV_HALO + tm] = a_ref[...].reshape(tm, SUBLANES, LANES)

    ya_parts = []
    for g, w in enumerate(POOL_WINDOWS):
        cols = slice(g * POOL_GROUP_DIM, (g + 1) * POOL_GROUP_DIM)
        zz = zs[:, cols]
        win = zz
        shift = 1
        while shift < w:
            win = win + pltpu.roll(win, shift, axis=0)
            shift *= 2
        d = (win[POOL_HALO:] * (1.0 / w) - zz[POOL_HALO:]).astype(BF16)
        y = jnp.dot(d, wgrp_ref[g], preferred_element_type=F32)
        ya_parts.append((y * pscale_ref[:, cols]).astype(BF16))
    pooled = jnp.concatenate(ya_parts, axis=-1)
    y_a = jnp.dot(pooled, wpo_ref[...], preferred_element_type=F32)

    first_tap = CONV_HALO - (CONV_KERNEL - 1)
    for r0 in range(0, tm, CONV_STEPS):
        acc = jnp.broadcast_to(bdw_ref[...][None], (CONV_STEPS, SUBLANES, LANES))
        for k in range(CONV_KERNEL):
            lo = r0 + first_tap + k
            acc = acc + cs[lo:lo + CONV_STEPS] * wdw_ref[k][None]
        conv_scr[r0:r0 + CONV_STEPS] = acc

    c = conv_scr[...].reshape(tm, CONV_WIDTH)
    mu = jnp.mean(c, axis=-1, keepdims=True)
    cc = c - mu
    var = jnp.mean(cc * cc, axis=-1, keepdims=True)
    ln = cc * lax.rsqrt(var + LN_EPS) * lng_ref[...] + lnb_ref[...]
    act = (ln * _sigmoid(ln)).astype(BF16)
    y_b = jnp.dot(act, wco_ref[...], preferred_element_type=F32)

    m = sga_ref[...].astype(F32) * y_a + sgb_ref[...].astype(F32) * y_b
    mo = jnp.dot(m.astype(BF16), wo_ref[...], preferred_element_type=F32)
    o_ref[...] = x_ref[...] + _rms_norm(mo, gpm_ref[...])


def _mixer(z, a, sga, sgb, x2d, z_meta, a_meta, w_grp, pool_scale, w_pool_out,
           w_dw, b_dw, ln_g, ln_b, w_conv_out, w_o, g_post, *, batch, seq):
    tm = MIX_TM
    nt = seq // tm
    row = lambda b, t: (b * nt + t, 0)
    const2 = lambda b, t: (0, 0)
    const3 = lambda b, t: (0, 0, 0)
    once = pl.Buffered(1)
    return pl.pallas_call(
        _mixer_kernel,
        out_shape=jax.ShapeDtypeStruct((batch * seq, D_MODEL), F32),
        grid=(batch, nt),
        in_specs=[
            pl.BlockSpec((tm, POOL_WIDTH), row),
            pl.BlockSpec((tm, CONV_WIDTH), row),
            pl.BlockSpec((tm, D_MODEL), row),
            pl.BlockSpec((tm, D_MODEL), row),
            pl.BlockSpec((tm, D_MODEL), row),
            pl.BlockSpec((N_META, POOL_WIDTH), const2, pipeline_mode=once),
            pl.BlockSpec((N_META, CONV_WIDTH), const2, pipeline_mode=once),
            pl.BlockSpec((len(POOL_WINDOWS), POOL_GROUP_DIM, POOL_GROUP_DIM), const3,
                         pipeline_mode=once),
            pl.BlockSpec((1, POOL_WIDTH), const2, pipeline_mode=once),
            pl.BlockSpec((POOL_WIDTH, D_MODEL), const2, pipeline_mode=once),
            pl.BlockSpec((CONV_KERNEL, SUBLANES, LANES), const3, pipeline_mode=once),
            pl.BlockSpec((SUBLANES, LANES), const2, pipeline_mode=once),
            pl.BlockSpec((1, CONV_WIDTH), const2, pipeline_mode=once),
            pl.BlockSpec((1, CONV_WIDTH), const2, pipeline_mode=once),
            pl.BlockSpec((CONV_WIDTH, D_MODEL), const2, pipeline_mode=once),
            pl.BlockSpec((D_MODEL, D_MODEL), const2, pipeline_mode=once),
            pl.BlockSpec((1, D_MODEL), const2, pipeline_mode=once),
        ],
        out_specs=pl.BlockSpec((tm, D_MODEL), row),
        scratch_shapes=[
            pltpu.VMEM((POOL_HALO + tm, POOL_WIDTH), F32),
            pltpu.VMEM((CONV_HALO + tm, SUBLANES, LANES), F32),
            pltpu.VMEM((tm, SUBLANES, LANES), F32),
        ],
        compiler_params=pltpu.CompilerParams(
            dimension_semantics=("arbitrary", "arbitrary"),
            vmem_limit_bytes=V7X_VMEM_LIMIT_BYTES),
        name="mixer",
    )(z, a, sga, sgb, x2d, z_meta, a_meta, w_grp, pool_scale, w_pool_out,
      w_dw, b_dw, ln_g, ln_b, w_conv_out, w_o, g_post)


def _mlp_step(h_ref, wup_ref, wdn_ref, o_ref, g1_ref, g2_ref, u_scr):
    k = pl.program_id(1)
    last = D_FF // MLP_TF - 1
    tm = h_ref.shape[0]
    mb = tm // MLP_ROW_BLOCKS
    row_blocks = [slice(r0, r0 + mb) for r0 in range(0, tm, mb)]

    def partial_out(rb):
        up = jnp.dot(u_scr[rb, :], wup_ref[...], preferred_element_type=F32)
        act = jnp.square(jnp.maximum(up, 0.0)).astype(BF16)
        return jnp.dot(act, wdn_ref[...], preferred_element_type=F32)

    def norm_in(rb):
        u_scr[rb, :] = _rms_norm(h_ref[rb, :], g1_ref[...]).astype(BF16)

    @pl.when(k == 0)
    def _():
        norm_in(row_blocks[0])
        for i, rb in enumerate(row_blocks):
            if i + 1 < len(row_blocks):
                norm_in(row_blocks[i + 1])
            o_ref[rb, :] = partial_out(rb)

    @pl.when(jnp.logical_and(k > 0, k < last))
    def _():
        o_ref[...] += partial_out(slice(None))

    @pl.when(k == last)
    def _():
        for rb in row_blocks:
            f = o_ref[rb, :] + partial_out(rb)
            o_ref[rb, :] = h_ref[rb, :] + _rms_norm(f, g2_ref[...])


def _mlp_kernel(h_hbm, g1_ref, wup_hbm, wdn_hbm, g2_ref, o_hbm, u_scr):
    m = h_hbm.shape[0]
    row = lambda i, k: (i, 0)
    pltpu.emit_pipeline(
        _mlp_step,
        grid=(m // MLP_TM, D_FF // MLP_TF),
        in_specs=[
            pl.BlockSpec((MLP_TM, D_MODEL), row,
                         pipeline_mode=pl.Buffered(2, use_lookahead=True)),
            pl.BlockSpec((D_MODEL, MLP_TF), lambda i, k: (0, k)),
            pl.BlockSpec((MLP_TF, D_MODEL), lambda i, k: (k, 0)),
        ],
        out_specs=[pl.BlockSpec((MLP_TM, D_MODEL), row)],
    )(h_hbm, wup_hbm, wdn_hbm, o_hbm, scratches=(g1_ref, g2_ref, u_scr))


def _mlp(h, g_pre, w_up, w_down, g_post):
    m = h.shape[0]
    hbm = pl.BlockSpec(memory_space=pl.ANY)
    vmem = pl.BlockSpec(memory_space=pltpu.VMEM)
    return pl.pallas_call(
        _mlp_kernel,
        out_shape=jax.ShapeDtypeStruct((m, D_MODEL), F32),
        in_specs=[hbm, vmem, hbm, hbm, vmem],
        out_specs=hbm,
        scratch_shapes=[pltpu.VMEM((MLP_TM, D_MODEL), BF16)],
        compiler_params=pltpu.CompilerParams(vmem_limit_bytes=V7X_VMEM_LIMIT_BYTES),
        name="mlp",
    )(h, g_pre, w_up, w_down, g_post)


def kernel(x, meta, g_pre_mix, w_in, w_pool_grp, pool_scale, w_pool_out, w_dw, b_dw, conv_ln_g, conv_ln_b, w_conv_out, w_o, g_post_mix, g_pre_mlp, w_up, w_down, g_post_mlp):
    batch, seq, _ = x.shape
    depth = w_in.shape[0]
    h = x.reshape(batch * seq, D_MODEL)
    meta_h = meta.astype(x.dtype)
    assert depth == 1, "meta rows are only carried as first-layer causal history"
    for l in range(depth):
        w_in_l = w_in[l].astype(BF16)
        g_pre = g_pre_mix[l][None]
        z_meta, a_meta = _meta_proj(meta_h, g_pre, w_in_l)
        (z, a, sga, sgb), (w_pool_out_b, w_conv_out_b, w_o_b, w_up_b, w_down_b) = _in_proj(
            h, g_pre, w_in_l, [w_pool_out[l], w_conv_out[l], w_o[l], w_up[l], w_down[l]])
        h = _mixer(z, a, sga, sgb, h, z_meta, a_meta,
                   w_pool_grp[l].astype(BF16), pool_scale[l][None], w_pool_out_b,
                   w_dw[l].reshape(CONV_KERNEL, SUBLANES, LANES),
                   b_dw[l].reshape(SUBLANES, LANES), conv_ln_g[l][None], conv_ln_b[l][None],
                   w_conv_out_b, w_o_b, g_post_mix[l][None], batch=batch, seq=seq)
        h = _mlp(h, g_pre_mlp[l][None], w_up_b, w_down_b, g_post_mlp[l][None])
    return h.reshape(batch, seq, D_MODEL)
```

```python
import jax
import jax.numpy as jnp
from jax import lax
from jax.experimental import pallas as pl
from jax.experimental.pallas import tpu as pltpu

D_MODEL = 2048
N_META = 16
POOL_WIDTH = D_MODEL // 2
POOL_WINDOWS = (2, 4, 8, 16)
POOL_GROUP_DIM = POOL_WIDTH // len(POOL_WINDOWS)
CONV_WIDTH = D_MODEL // 2
CONV_KERNEL = 31
D_FF = 4 * D_MODEL
IN_COLS = POOL_WIDTH + 2 * CONV_WIDTH + 2 * D_MODEL
RMS_EPS = 1e-6
LN_EPS = 1e-5

V7X_VMEM_LIMIT_BYTES = 56 * 1024 * 1024
SUBLANES = 8
LANES = 128

F32 = jnp.float32
BF16 = jnp.bfloat16

IN_TM = 256
IN_CHUNK = 1024
COL_V = POOL_WIDTH
COL_G = COL_V + CONV_WIDTH
COL_GA = COL_G + CONV_WIDTH
COL_GB = COL_GA + D_MODEL

MIX_TM = 256
POOL_HALO = 16
CONV_HALO = 32
CONV_STEPS = 4
LN_ROWS = 32

MLP_TM = 512
MLP_TF = 2048
MLP_ROW_BLOCKS = 2


def _rms_norm(x, g):
    ms = jnp.mean(x * x, axis=-1, keepdims=True)
    return x * lax.rsqrt(ms + RMS_EPS) * g


def _sigmoid(x):
    return 0.5 * jnp.tanh(0.5 * x) + 0.5


def _exact_zero_like(x):
    bits = pltpu.bitcast(x, jnp.uint32)
    return pltpu.bitcast((bits >> 16) >> 16, F32)


def _meta_proj_kernel(x_ref, g_ref, w_ref, z_ref, a_ref):
    u = _rms_norm(x_ref[...], g_ref[...]).astype(BF16)

    def proj(c0):
        return jnp.dot(u, w_ref[:, c0:c0 + IN_CHUNK], preferred_element_type=F32)

    z_ref[...] = proj(0)
    a_ref[...] = proj(COL_V) * _sigmoid(proj(COL_G))


def _meta_proj(meta, g_pre, w_in):
    const = lambda i: (0, 0)
    return pl.pallas_call(
        _meta_proj_kernel,
        out_shape=[jax.ShapeDtypeStruct((N_META, POOL_WIDTH), F32),
                   jax.ShapeDtypeStruct((N_META, CONV_WIDTH), F32)],
        grid=(1,),
        in_specs=[
            pl.BlockSpec((N_META, D_MODEL), const),
            pl.BlockSpec((1, D_MODEL), const),
            pl.BlockSpec((D_MODEL, COL_GA), const),
        ],
        out_specs=[pl.BlockSpec((N_META, POOL_WIDTH), const),
                   pl.BlockSpec((N_META, CONV_WIDTH), const)],
        compiler_params=pltpu.CompilerParams(
            dimension_semantics=("arbitrary",),
            vmem_limit_bytes=V7X_VMEM_LIMIT_BYTES),
        name="meta_proj",
    )(meta, g_pre, w_in)


def _in_proj_kernel(x_ref, g_ref, w_ref, zm_ref, am_ref, wdw_ref, bdw_ref, lng_ref, lnb_ref,
                    *refs):
    n_cast = (len(refs) - 7) // 2
    cast_in = refs[:n_cast]
    d_ref, act_ref, sga_ref, sgb_ref = refs[n_cast:n_cast + 4]
    cast_out = refs[n_cast + 4:2 * n_cast + 4]
    zs, cs, conv_scr = refs[2 * n_cast + 4:]
    tm = x_ref.shape[0]
    t = pl.program_id(1)

    @pl.when(t == 0)
    def _():
        zs[0:POOL_HALO, :] = zm_ref[...]
        cs[0:CONV_HALO - N_META] = jnp.zeros((CONV_HALO - N_META, SUBLANES, LANES), F32)
        cs[CONV_HALO - N_META:CONV_HALO] = am_ref[...].reshape(N_META, SUBLANES, LANES)

    @pl.when(t > 0)
    def _():
        zs[0:POOL_HALO, :] = zs[tm:tm + POOL_HALO, :]
        cs[0:CONV_HALO] = cs[tm:tm + CONV_HALO]

    u = _rms_norm(x_ref[...], g_ref[...]).astype(BF16)

    def proj(c0):
        return jnp.dot(u, w_ref[:, c0:c0 + IN_CHUNK], preferred_element_type=F32)

    zs[POOL_HALO:POOL_HALO + tm, :] = proj(0)
    a = proj(COL_V) * _sigmoid(proj(COL_G))
    cs[CONV_HALO:CONV_HALO + tm] = a.reshape(tm, SUBLANES, LANES)

    for g, w in enumerate(POOL_WINDOWS):
        cols = slice(g * POOL_GROUP_DIM, (g + 1) * POOL_GROUP_DIM)
        zz = zs[:, cols]
        win = zz
        shift = 1
        while shift < w:
            win = win + pltpu.roll(win, shift, axis=0)
            shift *= 2
        d_ref[:, cols] = (win[POOL_HALO:] * (1.0 / w) - zz[POOL_HALO:]).astype(BF16)

    first_tap = CONV_HALO - (CONV_KERNEL - 1)
    gate_cols = [(gate_ref, col0 + c, c) for gate_ref, col0 in ((sga_ref, COL_GA), (sgb_ref, COL_GB))
                 for c in range(0, D_MODEL, IN_CHUNK)]
    conv_starts = list(range(0, tm, CONV_STEPS))
    per_gate = len(conv_starts) // len(gate_cols)
    prev = None
    for i, (gate_ref, wcol, c) in enumerate(gate_cols):
        p = proj(wcol)
        gate_ref[:, c:c + IN_CHUNK] = _sigmoid(p).astype(BF16)
        for j, r0 in enumerate(conv_starts[i * per_gate:(i + 1) * per_gate]):
            pr = (j * tm // per_gate) // SUBLANES * SUBLANES
            pc = (j % (IN_CHUNK // LANES)) * LANES
            token = p[pr:pr + SUBLANES, pc:pc + LANES]
            if prev is not None:
                token = token + prev
            hold = _exact_zero_like(token)
            acc = jnp.broadcast_to((bdw_ref[...] + hold)[None], (CONV_STEPS, SUBLANES, LANES))
            for k in range(CONV_KERNEL):
                lo = r0 + first_tap + k
                acc = acc + cs[lo:lo + CONV_STEPS] * wdw_ref[k][None]
            conv_scr[r0:r0 + CONV_STEPS] = acc
            prev = acc[CONV_STEPS - 1]
            if (r0 + CONV_STEPS) % LN_ROWS == 0:
                b0 = r0 + CONV_STEPS - LN_ROWS
                cb = conv_scr[b0:b0 + LN_ROWS].reshape(LN_ROWS, CONV_WIDTH)
                mu = jnp.mean(cb, axis=-1, keepdims=True)
                cc = cb - mu
                var = jnp.mean(cc * cc, axis=-1, keepdims=True)
                ln = cc * lax.rsqrt(var + LN_EPS) * lng_ref[...] + lnb_ref[...]
                act_ref[b0:b0 + LN_ROWS, :] = (ln * _sigmoid(ln)).astype(BF16)

    for src, dst in zip(cast_in, cast_out):
        dst[...] = src[...].astype(BF16)


def _in_proj(x2d, g_pre, w_in, z_meta, a_meta, w_dw, b_dw, ln_g, ln_b, later_weights, *,
             batch, seq):
    tm = IN_TM
    nt = seq // tm
    m = batch * seq
    n_steps = batch * nt
    row = lambda b, t: (b * nt + t, 0)
    const2 = lambda b, t: (0, 0)
    const3 = lambda b, t: (0, 0, 0)
    once = pl.Buffered(1)
    cast_specs = [pl.BlockSpec((w.shape[0] // n_steps, w.shape[1]), row) for w in later_weights]
    outs = pl.pallas_call(
        _in_proj_kernel,
        out_shape=[jax.ShapeDtypeStruct((m, POOL_WIDTH), BF16),
                   jax.ShapeDtypeStruct((m, CONV_WIDTH), BF16),
                   jax.ShapeDtypeStruct((m, D_MODEL), BF16),
                   jax.ShapeDtypeStruct((m, D_MODEL), BF16)]
                  + [jax.ShapeDtypeStruct(w.shape, BF16) for w in later_weights],
        grid=(batch, nt),
        in_specs=[
            pl.BlockSpec((tm, D_MODEL), row),
            pl.BlockSpec((1, D_MODEL), const2, pipeline_mode=once),
            pl.BlockSpec((D_MODEL, IN_COLS), const2, pipeline_mode=once),
            pl.BlockSpec((N_META, POOL_WIDTH), const2, pipeline_mode=once),
            pl.BlockSpec((N_META, CONV_WIDTH), const2, pipeline_mode=once),
            pl.BlockSpec((CONV_KERNEL, SUBLANES, LANES), const3, pipeline_mode=once),
            pl.BlockSpec((SUBLANES, LANES), const2, pipeline_mode=once),
            pl.BlockSpec((1, CONV_WIDTH), const2, pipeline_mode=once),
            pl.BlockSpec((1, CONV_WIDTH), const2, pipeline_mode=once),
        ] + cast_specs,
        out_specs=[pl.BlockSpec((tm, POOL_WIDTH), row), pl.BlockSpec((tm, CONV_WIDTH), row),
                   pl.BlockSpec((tm, D_MODEL), row), pl.BlockSpec((tm, D_MODEL), row)] + cast_specs,
        scratch_shapes=[
            pltpu.VMEM((POOL_HALO + tm, POOL_WIDTH), F32),
            pltpu.VMEM((CONV_HALO + tm, SUBLANES, LANES), F32),
            pltpu.VMEM((tm, SUBLANES, LANES), F32),
        ],
        compiler_params=pltpu.CompilerParams(
            dimension_semantics=("arbitrary", "arbitrary"),
            vmem_limit_bytes=V7X_VMEM_LIMIT_BYTES),
        name="in_proj",
    )(x2d, g_pre, w_in, z_meta, a_meta, w_dw, b_dw, ln_g, ln_b, *later_weights)
    return outs[:4], outs[4:]


def _mixer_kernel(d_ref, act_ref, sga_ref, sgb_ref, x_ref,
                  wgrp_ref, pscale_ref, wpo_ref, wco_ref, wo_ref, gpm_ref, *refs):
    n_cast = (len(refs) - 1) // 2
    cast_in = refs[:n_cast]
    o_ref = refs[n_cast]
    cast_out = refs[n_cast + 1:]
    ya_parts = []
    for g in range(len(POOL_WINDOWS)):
        cols = slice(g * POOL_GROUP_DIM, (g + 1) * POOL_GROUP_DIM)
        y = jnp.dot(d_ref[:, cols], wgrp_ref[g], preferred_element_type=F32)
        ya_parts.append((y * pscale_ref[:, cols]).astype(BF16))
    pooled = jnp.concatenate(ya_parts, axis=-1)
    y_a = jnp.dot(pooled, wpo_ref[...], preferred_element_type=F32)
    y_b = jnp.dot(act_ref[...], wco_ref[...], preferred_element_type=F32)
    m = sga_ref[...].astype(F32) * y_a + sgb_ref[...].astype(F32) * y_b
    mo = jnp.dot(m.astype(BF16), wo_ref[...], preferred_element_type=F32)
    o_ref[...] = x_ref[...] + _rms_norm(mo, gpm_ref[...])

    for src, dst in zip(cast_in, cast_out):
        dst[...] = src[...].astype(BF16)


def _mixer(d, act, sga, sgb, x2d, w_grp, pool_scale, w_pool_out, w_conv_out, w_o, g_post,
           later_weights):
    tm = MIX_TM
    m = x2d.shape[0]
    n_steps = m // tm
    row = lambda i: (i, 0)
    cast_specs = [pl.BlockSpec((w.shape[0] // n_steps, w.shape[1]), lambda i: (i, 0))
                  for w in later_weights]
    const2 = lambda i: (0, 0)
    const3 = lambda i: (0, 0, 0)
    once = pl.Buffered(1)
    outs = pl.pallas_call(
        _mixer_kernel,
        out_shape=[jax.ShapeDtypeStruct((m, D_MODEL), F32)]
                  + [jax.ShapeDtypeStruct(w.shape, BF16) for w in later_weights],
        grid=(n_steps,),
        in_specs=[
            pl.BlockSpec((tm, POOL_WIDTH), row),
            pl.BlockSpec((tm, CONV_WIDTH), row),
            pl.BlockSpec((tm, D_MODEL), row),
            pl.BlockSpec((tm, D_MODEL), row),
            pl.BlockSpec((tm, D_MODEL), row),
            pl.BlockSpec((len(POOL_WINDOWS), POOL_GROUP_DIM, POOL_GROUP_DIM), const3,
                         pipeline_mode=once),
            pl.BlockSpec((1, POOL_WIDTH), const2, pipeline_mode=once),
            pl.BlockSpec((POOL_WIDTH, D_MODEL), const2, pipeline_mode=once),
            pl.BlockSpec((CONV_WIDTH, D_MODEL), const2, pipeline_mode=once),
            pl.BlockSpec((D_MODEL, D_MODEL), const2, pipeline_mode=once),
            pl.BlockSpec((1, D_MODEL), const2, pipeline_mode=once),
        ] + cast_specs,
        out_specs=[pl.BlockSpec((tm, D_MODEL), row)] + cast_specs,
        compiler_params=pltpu.CompilerParams(
            dimension_semantics=("arbitrary",),
            vmem_limit_bytes=V7X_VMEM_LIMIT_BYTES),
        name="mixer",
    )(d, act, sga, sgb, x2d, w_grp, pool_scale, w_pool_out, w_conv_out, w_o, g_post,
      *later_weights)
    return outs[0], outs[1:]


def _mlp_step(h_ref, wup_ref, wdn_ref, o_ref, g1_ref, g2_ref, u_scr):
    k = pl.program_id(1)
    last = D_FF // MLP_TF - 1
    tm = h_ref.shape[0]
    mb = tm // MLP_ROW_BLOCKS
    row_blocks = [slice(r0, r0 + mb) for r0 in range(0, tm, mb)]

    def partial_out(rb):
        up = jnp.dot(u_scr[rb, :], wup_ref[...], preferred_element_type=F32)
        act = jnp.square(jnp.maximum(up, 0.0)).astype(BF16)
        return jnp.dot(act, wdn_ref[...], preferred_element_type=F32)

    def norm_in(rb):
        u_scr[rb, :] = _rms_norm(h_ref[rb, :], g1_ref[...]).astype(BF16)

    @pl.when(k == 0)
    def _():
        norm_in(row_blocks[0])
        for i, rb in enumerate(row_blocks):
            if i + 1 < len(row_blocks):
                norm_in(row_blocks[i + 1])
            o_ref[rb, :] = partial_out(rb)

    @pl.when(jnp.logical_and(k > 0, k < last))
    def _():
        o_ref[...] += partial_out(slice(None))

    @pl.when(k == last)
    def _():
        for rb in row_blocks:
            f = o_ref[rb, :] + partial_out(rb)
            o_ref[rb, :] = h_ref[rb, :] + _rms_norm(f, g2_ref[...])


def _mlp_kernel(h_hbm, g1_ref, wup_hbm, wdn_hbm, g2_ref, o_hbm, u_scr):
    m = h_hbm.shape[0]
    row = lambda i, k: (i, 0)
    pltpu.emit_pipeline(
        _mlp_step,
        grid=(m // MLP_TM, D_FF // MLP_TF),
        in_specs=[
            pl.BlockSpec((MLP_TM, D_MODEL), row,
                         pipeline_mode=pl.Buffered(2, use_lookahead=True)),
            pl.BlockSpec((D_MODEL, MLP_TF), lambda i, k: (0, k)),
            pl.BlockSpec((MLP_TF, D_MODEL), lambda i, k: (k, 0)),
        ],
        out_specs=[pl.BlockSpec((MLP_TM, D_MODEL), row)],
    )(h_hbm, wup_hbm, wdn_hbm, o_hbm, scratches=(g1_ref, g2_ref, u_scr))


def _mlp(h, g_pre, w_up, w_down, g_post):
    m = h.shape[0]
    hbm = pl.BlockSpec(memory_space=pl.ANY)
    vmem = pl.BlockSpec(memory_space=pltpu.VMEM)
    return pl.pallas_call(
        _mlp_kernel,
        out_shape=jax.ShapeDtypeStruct((m, D_MODEL), F32),
        in_specs=[hbm, vmem, hbm, hbm, vmem],
        out_specs=hbm,
        scratch_shapes=[pltpu.VMEM((MLP_TM, D_MODEL), BF16)],
        compiler_params=pltpu.CompilerParams(vmem_limit_bytes=V7X_VMEM_LIMIT_BYTES),
        name="mlp",
    )(h, g_pre, w_up, w_down, g_post)


def kernel(x, meta, g_pre_mix, w_in, w_pool_grp, pool_scale, w_pool_out, w_dw, b_dw, conv_ln_g, conv_ln_b, w_conv_out, w_o, g_post_mix, g_pre_mlp, w_up, w_down, g_post_mlp):
    batch, seq, _ = x.shape
    depth = w_in.shape[0]
    h = x.reshape(batch * seq, D_MODEL)
    meta_h = meta.astype(x.dtype)
    assert depth == 1, "meta rows are only carried as first-layer causal history"
    for l in range(depth):
        w_in_l = w_in[l].astype(BF16)
        g_pre = g_pre_mix[l][None]
        z_meta, a_meta = _meta_proj(meta_h, g_pre, w_in_l)
        (d, act, sga, sgb), (w_pool_out_b, w_conv_out_b, w_o_b) = _in_proj(
            h, g_pre, w_in_l, z_meta, a_meta,
            w_dw[l].reshape(CONV_KERNEL, SUBLANES, LANES), b_dw[l].reshape(SUBLANES, LANES),
            conv_ln_g[l][None], conv_ln_b[l][None],
            [w_pool_out[l], w_conv_out[l], w_o[l]], batch=batch, seq=seq)
        h, (w_up_b, w_down_b) = _mixer(
            d, act, sga, sgb, h, w_pool_grp[l].astype(BF16), pool_scale[l][None],
            w_pool_out_b, w_conv_out_b, w_o_b, g_post_mix[l][None], [w_up[l], w_down[l]])
        h = _mlp(h, g_pre_mlp[l][None], w_up_b, w_down_b, g_post_mlp[l][None])
    return h.reshape(batch, seq, D_MODEL)
```

```python
import jax
import jax.numpy as jnp
from jax import lax
from jax.experimental import pallas as pl
from jax.experimental.pallas import tpu as pltpu

D_MODEL = 2048
N_META = 16
POOL_WIDTH = D_MODEL // 2
POOL_WINDOWS = (2, 4, 8, 16)
POOL_GROUP_DIM = POOL_WIDTH // len(POOL_WINDOWS)
CONV_WIDTH = D_MODEL // 2
CONV_KERNEL = 31
D_FF = 4 * D_MODEL
IN_COLS = POOL_WIDTH + 2 * CONV_WIDTH + 2 * D_MODEL
RMS_EPS = 1e-6
LN_EPS = 1e-5

V7X_VMEM_LIMIT_BYTES = 56 * 1024 * 1024
SUBLANES = 8
LANES = 128

F32 = jnp.float32
BF16 = jnp.bfloat16

IN_TM = 256
IN_CHUNK = 1024
IN_W_ROWS = 32
COL_V = POOL_WIDTH
COL_G = COL_V + CONV_WIDTH
COL_GA = COL_G + CONV_WIDTH
COL_GB = COL_GA + D_MODEL

MIX_TM = 256
POOL_HALO = 16
CONV_HALO = 32
CONV_STEPS = 8

MLP_TM = 512
MLP_TF = 2048
MLP_ROW_BLOCKS = 2


def _rms_norm(x, g):
    ms = jnp.mean(x * x, axis=-1, keepdims=True)
    return x * lax.rsqrt(ms + RMS_EPS) * g


def _sigmoid(x):
    return 0.5 * jnp.tanh(0.5 * x) + 0.5


def _meta_proj_kernel(x_ref, g_ref, w_ref, z_ref, a_ref):
    j = pl.program_id(0)
    u = _rms_norm(x_ref[...], g_ref[...]).astype(BF16)
    p = jnp.dot(u, w_ref[...].astype(BF16), preferred_element_type=F32)

    @pl.when(j == 0)
    def _():
        z_ref[...] = p

    @pl.when(j == 1)
    def _():
        a_ref[...] = p

    @pl.when(j == 2)
    def _():
        a_ref[...] = a_ref[...] * _sigmoid(p)


def _meta_proj(meta, g_pre, w_in):
    const = lambda i: (0, 0)
    return pl.pallas_call(
        _meta_proj_kernel,
        out_shape=[jax.ShapeDtypeStruct((N_META, POOL_WIDTH), F32),
                   jax.ShapeDtypeStruct((N_META, CONV_WIDTH), F32)],
        grid=(COL_GA // IN_CHUNK,),
        in_specs=[
            pl.BlockSpec((N_META, D_MODEL), const),
            pl.BlockSpec((1, D_MODEL), const),
            pl.BlockSpec((D_MODEL, IN_CHUNK), lambda j: (0, j)),
        ],
        out_specs=[pl.BlockSpec((N_META, POOL_WIDTH), const),
                   pl.BlockSpec((N_META, CONV_WIDTH), const)],
        compiler_params=pltpu.CompilerParams(
            dimension_semantics=("arbitrary",),
            vmem_limit_bytes=V7X_VMEM_LIMIT_BYTES),
        name="meta_proj",
    )(meta, g_pre, w_in)


def _in_proj_kernel(x_ref, g_ref, w_hbm, *refs):
    n_cast = (len(refs) - 7) // 2
    cast_in = refs[:n_cast]
    z_ref, a_ref, sga_ref, sgb_ref = refs[n_cast:n_cast + 4]
    cast_out = refs[n_cast + 4:2 * n_cast + 4]
    w_ref, stage, sem = refs[2 * n_cast + 4:]

    @pl.when(pl.program_id(0) == 0)
    def _():
        n_chunks = D_MODEL // IN_W_ROWS

        def copy(c, slot):
            return pltpu.make_async_copy(
                w_hbm.at[pl.ds(c * IN_W_ROWS, IN_W_ROWS)], stage.at[slot], sem.at[slot])

        copy(0, 0).start()

        def body(c, carry):
            slot = c % 2

            @pl.when(c + 1 < n_chunks)
            def _():
                copy(c + 1, 1 - slot).start()

            copy(c, slot).wait()
            r0 = pl.multiple_of(c * IN_W_ROWS, IN_W_ROWS)
            w_ref[pl.ds(r0, IN_W_ROWS), :] = stage[slot].astype(BF16)
            return carry

        lax.fori_loop(0, n_chunks, body, 0)

    u = _rms_norm(x_ref[...], g_ref[...]).astype(BF16)

    def proj(c0):
        return jnp.dot(u, w_ref[:, c0:c0 + IN_CHUNK], preferred_element_type=F32)

    for c in range(0, POOL_WIDTH, IN_CHUNK):
        z_ref[:, c:c + IN_CHUNK] = proj(c)
    for c in range(0, CONV_WIDTH, IN_CHUNK):
        a_ref[:, c:c + IN_CHUNK] = proj(COL_V + c) * _sigmoid(proj(COL_G + c))
    for gate_ref, col0 in ((sga_ref, COL_GA), (sgb_ref, COL_GB)):
        for c in range(0, D_MODEL, IN_CHUNK):
            gate_ref[:, c:c + IN_CHUNK] = _sigmoid(proj(col0 + c)).astype(BF16)

    for src, dst in zip(cast_in, cast_out):
        dst[...] = src[...].astype(BF16)


def _in_proj(x2d, g_pre, w_in, later_weights):
    tm = IN_TM
    m = x2d.shape[0]
    n_steps = m // tm
    row = lambda i: (i, 0)
    const = lambda i: (0, 0)
    once = pl.Buffered(1)
    cast_specs = [pl.BlockSpec((w.shape[0] // n_steps, w.shape[1]), row) for w in later_weights]
    outs = pl.pallas_call(
        _in_proj_kernel,
        out_shape=[jax.ShapeDtypeStruct((m, POOL_WIDTH), F32),
                   jax.ShapeDtypeStruct((m, CONV_WIDTH), F32),
                   jax.ShapeDtypeStruct((m, D_MODEL), BF16),
                   jax.ShapeDtypeStruct((m, D_MODEL), BF16)]
                  + [jax.ShapeDtypeStruct(w.shape, BF16) for w in later_weights],
        grid=(n_steps,),
        in_specs=[
            pl.BlockSpec((tm, D_MODEL), row),
            pl.BlockSpec((1, D_MODEL), const, pipeline_mode=once),
            pl.BlockSpec(memory_space=pl.ANY),
        ] + cast_specs,
        out_specs=[pl.BlockSpec((tm, POOL_WIDTH), row), pl.BlockSpec((tm, CONV_WIDTH), row),
                   pl.BlockSpec((tm, D_MODEL), row), pl.BlockSpec((tm, D_MODEL), row)] + cast_specs,
        scratch_shapes=[pltpu.VMEM((D_MODEL, IN_COLS), BF16),
                        pltpu.VMEM((2, IN_W_ROWS, IN_COLS), F32),
                        pltpu.SemaphoreType.DMA((2,))],
        compiler_params=pltpu.CompilerParams(
            dimension_semantics=("arbitrary",),
            vmem_limit_bytes=V7X_VMEM_LIMIT_BYTES),
        name="in_proj",
    )(x2d, g_pre, w_in, *later_weights)
    return outs[:4], outs[4:]


def _mixer_kernel(z_ref, a_ref, sga_ref, sgb_ref, x_ref, zm_ref, am_ref,
                  wgrp_ref, pscale_ref, wpo_ref, wdw_ref, bdw_ref,
                  lng_ref, lnb_ref, wco_ref, wo_ref, gpm_ref,
                  o_ref, zs, cs, conv_scr):
    tm = z_ref.shape[0]
    t = pl.program_id(1)

    @pl.when(t == 0)
    def _():
        zs[0:POOL_HALO, :] = zm_ref[...]
        cs[0:CONV_HALO - N_META] = jnp.zeros((CONV_HALO - N_META, SUBLANES, LANES), F32)
        cs[CONV_HALO - N_META:CONV_HALO] = am_ref[...].reshape(N_META, SUBLANES, LANES)

    @pl.when(t > 0)
    def _():
        zs[0:POOL_HALO, :] = zs[tm:tm + POOL_HALO, :]
        cs[0:CONV_HALO] = cs[tm:tm + CONV_HALO]

    zs[POOL_HALO:POOL_HALO + tm, :] = z_ref[...]
    cs[CONV_HALO:CONV_HALO + tm] = a_ref[...].reshape(tm, SUBLANES, LANES)

    ya_parts = []
    for g, w in enumerate(POOL_WINDOWS):
        cols = slice(g * POOL_GROUP_DIM, (g + 1) * POOL_GROUP_DIM)
        zz = zs[:, cols]
        win = zz
        shift = 1
        while shift < w:
            win = win + pltpu.roll(win, shift, axis=0)
            shift *= 2
        d = (win[POOL_HALO:] * (1.0 / w) - zz[POOL_HALO:]).astype(BF16)
        y = jnp.dot(d, wgrp_ref[g], preferred_element_type=F32)
        ya_parts.append((y * pscale_ref[:, cols]).astype(BF16))
    pooled = jnp.concatenate(ya_parts, axis=-1)
    y_a = jnp.dot(pooled, wpo_ref[...], preferred_element_type=F32)

    first_tap = CONV_HALO - (CONV_KERNEL - 1)
    for r0 in range(0, tm, CONV_STEPS):
        acc = jnp.broadcast_to(bdw_ref[...][None], (CONV_STEPS, SUBLANES, LANES))
        for k in range(CONV_KERNEL):
            lo = r0 + first_tap + k
            acc = acc + cs[lo:lo + CONV_STEPS] * wdw_ref[k][None]
        conv_scr[r0:r0 + CONV_STEPS] = acc

    c = conv_scr[...].reshape(tm, CONV_WIDTH)
    mu = jnp.mean(c, axis=-1, keepdims=True)
    cc = c - mu
    var = jnp.mean(cc * cc, axis=-1, keepdims=True)
    ln = cc * lax.rsqrt(var + LN_EPS) * lng_ref[...] + lnb_ref[...]
    act = (ln * _sigmoid(ln)).astype(BF16)
    y_b = jnp.dot(act, wco_ref[...], preferred_element_type=F32)

    m = sga_ref[...].astype(F32) * y_a + sgb_ref[...].astype(F32) * y_b
    mo = jnp.dot(m.astype(BF16), wo_ref[...], preferred_element_type=F32)
    o_ref[...] = x_ref[...] + _rms_norm(mo, gpm_ref[...])


def _mixer(z, a, sga, sgb, x2d, z_meta, a_meta, w_grp, pool_scale, w_pool_out,
           w_dw, b_dw, ln_g, ln_b, w_conv_out, w_o, g_post, *, batch, seq):
    tm = MIX_TM
    nt = seq // tm
    row = lambda b, t: (b * nt + t, 0)
    const2 = lambda b, t: (0, 0)
    const3 = lambda b, t: (0, 0, 0)
    once = pl.Buffered(1)
    return pl.pallas_call(
        _mixer_kernel,
        out_shape=jax.ShapeDtypeStruct((batch * seq, D_MODEL), F32),
        grid=(batch, nt),
        in_specs=[
            pl.BlockSpec((tm, POOL_WIDTH), row),
            pl.BlockSpec((tm, CONV_WIDTH), row),
            pl.BlockSpec((tm, D_MODEL), row),
            pl.BlockSpec((tm, D_MODEL), row),
            pl.BlockSpec((tm, D_MODEL), row),
            pl.BlockSpec((N_META, POOL_WIDTH), const2, pipeline_mode=once),
            pl.BlockSpec((N_META, CONV_WIDTH), const2, pipeline_mode=once),
            pl.BlockSpec((len(POOL_WINDOWS), POOL_GROUP_DIM, POOL_GROUP_DIM), const3,
                         pipeline_mode=once),
            pl.BlockSpec((1, POOL_WIDTH), const2, pipeline_mode=once),
            pl.BlockSpec((POOL_WIDTH, D_MODEL), const2, pipeline_mode=once),
            pl.BlockSpec((CONV_KERNEL, SUBLANES, LANES), const3, pipeline_mode=once),
            pl.BlockSpec((SUBLANES, LANES), const2, pipeline_mode=once),
            pl.BlockSpec((1, CONV_WIDTH), const2, pipeline_mode=once),
            pl.BlockSpec((1, CONV_WIDTH), const2, pipeline_mode=once),
            pl.BlockSpec((CONV_WIDTH, D_MODEL), const2, pipeline_mode=once),
            pl.BlockSpec((D_MODEL, D_MODEL), const2, pipeline_mode=once),
            pl.BlockSpec((1, D_MODEL), const2, pipeline_mode=once),
        ],
        out_specs=pl.BlockSpec((tm, D_MODEL), row),
        scratch_shapes=[
            pltpu.VMEM((POOL_HALO + tm, POOL_WIDTH), F32),
            pltpu.VMEM((CONV_HALO + tm, SUBLANES, LANES), F32),
            pltpu.VMEM((tm, SUBLANES, LANES), F32),
        ],
        compiler_params=pltpu.CompilerParams(
            dimension_semantics=("arbitrary", "arbitrary"),
            vmem_limit_bytes=V7X_VMEM_LIMIT_BYTES),
        name="mixer",
    )(z, a, sga, sgb, x2d, z_meta, a_meta, w_grp, pool_scale, w_pool_out,
      w_dw, b_dw, ln_g, ln_b, w_conv_out, w_o, g_post)


def _mlp_step(h_ref, wup_ref, wdn_ref, o_ref, g1_ref, g2_ref, u_scr):
    k = pl.program_id(1)
    last = D_FF // MLP_TF - 1
    tm = h_ref.shape[0]
    mb = tm // MLP_ROW_BLOCKS
    row_blocks = [slice(r0, r0 + mb) for r0 in range(0, tm, mb)]

    def partial_out(rb):
        up = jnp.dot(u_scr[rb, :], wup_ref[...], preferred_element_type=F32)
        act = jnp.square(jnp.maximum(up, 0.0)).astype(BF16)
        return jnp.dot(act, wdn_ref[...], preferred_element_type=F32)

    def norm_in(rb):
        u_scr[rb, :] = _rms_norm(h_ref[rb, :], g1_ref[...]).astype(BF16)

    @pl.when(k == 0)
    def _():
        norm_in(row_blocks[0])
        for i, rb in enumerate(row_blocks):
            if i + 1 < len(row_blocks):
                norm_in(row_blocks[i + 1])
            o_ref[rb, :] = partial_out(rb)

    @pl.when(jnp.logical_and(k > 0, k < last))
    def _():
        o_ref[...] += partial_out(slice(None))

    @pl.when(k == last)
    def _():
        for rb in row_blocks:
            f = o_ref[rb, :] + partial_out(rb)
            o_ref[rb, :] = h_ref[rb, :] + _rms_norm(f, g2_ref[...])


def _mlp_kernel(h_hbm, g1_ref, wup_hbm, wdn_hbm, g2_ref, o_hbm, u_scr):
    m = h_hbm.shape[0]
    row = lambda i, k: (i, 0)
    pltpu.emit_pipeline(
        _mlp_step,
        grid=(m // MLP_TM, D_FF // MLP_TF),
        in_specs=[
            pl.BlockSpec((MLP_TM, D_MODEL), row,
                         pipeline_mode=pl.Buffered(2, use_lookahead=True)),
            pl.BlockSpec((D_MODEL, MLP_TF), lambda i, k: (0, k)),
            pl.BlockSpec((MLP_TF, D_MODEL), lambda i, k: (k, 0)),
        ],
        out_specs=[pl.BlockSpec((MLP_TM, D_MODEL), row)],
    )(h_hbm, wup_hbm, wdn_hbm, o_hbm, scratches=(g1_ref, g2_ref, u_scr))


def _mlp(h, g_pre, w_up, w_down, g_post):
    m = h.shape[0]
    hbm = pl.BlockSpec(memory_space=pl.ANY)
    vmem = pl.BlockSpec(memory_space=pltpu.VMEM)
    return pl.pallas_call(
        _mlp_kernel,
        out_shape=jax.ShapeDtypeStruct((m, D_MODEL), F32),
        in_specs=[hbm, vmem, hbm, hbm, vmem],
        out_specs=hbm,
        scratch_shapes=[pltpu.VMEM((MLP_TM, D_MODEL), BF16)],
        compiler_params=pltpu.CompilerParams(vmem_limit_bytes=V7X_VMEM_LIMIT_BYTES),
        name="mlp",
    )(h, g_pre, w_up, w_down, g_post)


def kernel(x, meta, g_pre_mix, w_in, w_pool_grp, pool_scale, w_pool_out, w_dw, b_dw, conv_ln_g, conv_ln_b, w_conv_out, w_o, g_post_mix, g_pre_mlp, w_up, w_down, g_post_mlp):
    batch, seq, _ = x.shape
    depth = w_in.shape[0]
    h = x.reshape(batch * seq, D_MODEL)
    meta_h = meta.astype(x.dtype)
    assert depth == 1, "meta rows are only carried as first-layer causal history"
    for l in range(depth):
        w_in_l = w_in[l]
        g_pre = g_pre_mix[l][None]
        z_meta, a_meta = _meta_proj(meta_h, g_pre, w_in_l)
        (z, a, sga, sgb), (w_pool_out_b, w_conv_out_b, w_o_b, w_up_b, w_down_b) = _in_proj(
            h, g_pre, w_in_l, [w_pool_out[l], w_conv_out[l], w_o[l], w_up[l], w_down[l]])
        h = _mixer(z, a, sga, sgb, h, z_meta, a_meta,
                   w_pool_grp[l].astype(BF16), pool_scale[l][None], w_pool_out_b,
                   w_dw[l].reshape(CONV_KERNEL, SUBLANES, LANES),
                   b_dw[l].reshape(SUBLANES, LANES), conv_ln_g[l][None], conv_ln_b[l][None],
                   w_conv_out_b, w_o_b, g_post_mix[l][None], batch=batch, seq=seq)
        h = _mlp(h, g_pre_mlp[l][None], w_up_b, w_down_b, g_post_mlp[l][None])
    return h.reshape(batch, seq, D_MODEL)
```

```python
import jax
import jax.numpy as jnp
from jax import lax
from jax.experimental import pallas as pl
from jax.experimental.pallas import tpu as pltpu

D_MODEL = 2048
N_META = 16
POOL_WIDTH = D_MODEL // 2
POOL_WINDOWS = (2, 4, 8, 16)
POOL_GROUP_DIM = POOL_WIDTH // len(POOL_WINDOWS)
CONV_WIDTH = D_MODEL // 2
CONV_KERNEL = 31
D_FF = 4 * D_MODEL
IN_COLS = POOL_WIDTH + 2 * CONV_WIDTH + 2 * D_MODEL
RMS_EPS = 1e-6
LN_EPS = 1e-5

V7X_VMEM_LIMIT_BYTES = 56 * 1024 * 1024
SUBLANES = 8
LANES = 128

F32 = jnp.float32
BF16 = jnp.bfloat16

IN_TM = 256
IN_CHUNK = 1024
IN_W_ROWS = 32
IN_W_SLOTS = 4
COL_V = POOL_WIDTH
COL_G = COL_V + CONV_WIDTH
COL_GA = COL_G + CONV_WIDTH
COL_GB = COL_GA + D_MODEL

MIX_TM = 256
POOL_HALO = 16
CONV_HALO = 32
CONV_STEPS = 8

MLP_TM = 512
MLP_TF = 2048
MLP_ROW_BLOCKS = 2


def _rms_norm(x, g):
    ms = jnp.mean(x * x, axis=-1, keepdims=True)
    return x * lax.rsqrt(ms + RMS_EPS) * g


def _sigmoid(x):
    return 0.5 * jnp.tanh(0.5 * x) + 0.5


def _meta_proj_kernel(x_ref, g_ref, w_ref, z_ref, a_ref):
    j = pl.program_id(0)
    u = _rms_norm(x_ref[...], g_ref[...]).astype(BF16)
    p = jnp.dot(u, w_ref[...].astype(BF16), preferred_element_type=F32)

    @pl.when(j == 0)
    def _():
        z_ref[...] = p

    @pl.when(j == 1)
    def _():
        a_ref[...] = p

    @pl.when(j == 2)
    def _():
        a_ref[...] = a_ref[...] * _sigmoid(p)


def _meta_proj(meta, g_pre, w_in):
    const = lambda i: (0, 0)
    return pl.pallas_call(
        _meta_proj_kernel,
        out_shape=[jax.ShapeDtypeStruct((N_META, POOL_WIDTH), F32),
                   jax.ShapeDtypeStruct((N_META, CONV_WIDTH), F32)],
        grid=(COL_GA // IN_CHUNK,),
        in_specs=[
            pl.BlockSpec((N_META, D_MODEL), const),
            pl.BlockSpec((1, D_MODEL), const),
            pl.BlockSpec((D_MODEL, IN_CHUNK), lambda j: (0, j)),
        ],
        out_specs=[pl.BlockSpec((N_META, POOL_WIDTH), const),
                   pl.BlockSpec((N_META, CONV_WIDTH), const)],
        compiler_params=pltpu.CompilerParams(
            dimension_semantics=("arbitrary",),
            vmem_limit_bytes=V7X_VMEM_LIMIT_BYTES),
        name="meta_proj",
    )(meta, g_pre, w_in)


def _in_proj_kernel(x_ref, g_ref, w_hbm, *refs):
    n_cast = (len(refs) - 7) // 2
    cast_in = refs[:n_cast]
    z_ref, a_ref, sga_ref, sgb_ref = refs[n_cast:n_cast + 4]
    cast_out = refs[n_cast + 4:2 * n_cast + 4]
    w_ref, stage, sem = refs[2 * n_cast + 4:]

    @pl.when(pl.program_id(0) == 0)
    def _():
        n_chunks = D_MODEL // IN_W_ROWS

        def copy(c, slot):
            return pltpu.make_async_copy(
                w_hbm.at[pl.ds(c * IN_W_ROWS, IN_W_ROWS)], stage.at[slot], sem.at[slot])

        for c in range(IN_W_SLOTS - 1):
            copy(c, c).start()

        def body(c, carry):
            slot = c % IN_W_SLOTS
            ahead = c + IN_W_SLOTS - 1

            @pl.when(ahead < n_chunks)
            def _():
                copy(ahead, ahead % IN_W_SLOTS).start()

            copy(c, slot).wait()
            r0 = pl.multiple_of(c * IN_W_ROWS, IN_W_ROWS)
            w_ref[pl.ds(r0, IN_W_ROWS), :] = stage[slot].astype(BF16)
            return carry

        lax.fori_loop(0, n_chunks, body, 0)

    u = _rms_norm(x_ref[...], g_ref[...]).astype(BF16)

    def proj(c0):
        return jnp.dot(u, w_ref[:, c0:c0 + IN_CHUNK], preferred_element_type=F32)

    for c in range(0, POOL_WIDTH, IN_CHUNK):
        z_ref[:, c:c + IN_CHUNK] = proj(c)
    for c in range(0, CONV_WIDTH, IN_CHUNK):
        a_ref[:, c:c + IN_CHUNK] = proj(COL_V + c) * _sigmoid(proj(COL_G + c))
    for gate_ref, col0 in ((sga_ref, COL_GA), (sgb_ref, COL_GB)):
        for c in range(0, D_MODEL, IN_CHUNK):
            gate_ref[:, c:c + IN_CHUNK] = _sigmoid(proj(col0 + c)).astype(BF16)

    for src, dst in zip(cast_in, cast_out):
        dst[...] = src[...].astype(BF16)


def _in_proj(x2d, g_pre, w_in, later_weights):
    tm = IN_TM
    m = x2d.shape[0]
    n_steps = m // tm
    row = lambda i: (i, 0)
    const = lambda i: (0, 0)
    once = pl.Buffered(1)
    cast_specs = [pl.BlockSpec((w.shape[0] // n_steps, w.shape[1]), row) for w in later_weights]
    outs = pl.pallas_call(
        _in_proj_kernel,
        out_shape=[jax.ShapeDtypeStruct((m, POOL_WIDTH), F32),
                   jax.ShapeDtypeStruct((m, CONV_WIDTH), F32),
                   jax.ShapeDtypeStruct((m, D_MODEL), BF16),
                   jax.ShapeDtypeStruct((m, D_MODEL), BF16)]
                  + [jax.ShapeDtypeStruct(w.shape, BF16) for w in later_weights],
        grid=(n_steps,),
        in_specs=[
            pl.BlockSpec((tm, D_MODEL), row),
            pl.BlockSpec((1, D_MODEL), const, pipeline_mode=once),
            pl.BlockSpec(memory_space=pl.ANY),
        ] + cast_specs,
        out_specs=[pl.BlockSpec((tm, POOL_WIDTH), row), pl.BlockSpec((tm, CONV_WIDTH), row),
                   pl.BlockSpec((tm, D_MODEL), row), pl.BlockSpec((tm, D_MODEL), row)] + cast_specs,
        scratch_shapes=[pltpu.VMEM((D_MODEL, IN_COLS), BF16),
                        pltpu.VMEM((IN_W_SLOTS, IN_W_ROWS, IN_COLS), F32),
                        pltpu.SemaphoreType.DMA((IN_W_SLOTS,))],
        compiler_params=pltpu.CompilerParams(
            dimension_semantics=("arbitrary",),
            vmem_limit_bytes=V7X_VMEM_LIMIT_BYTES),
        name="in_proj",
    )(x2d, g_pre, w_in, *later_weights)
    return outs[:4], outs[4:]


def _mixer_kernel(z_ref, a_ref, sga_ref, sgb_ref, x_ref, zm_ref, am_ref,
                  wgrp_ref, pscale_ref, wpo_ref, wdw_ref, bdw_ref,
                  lng_ref, lnb_ref, wco_ref, wo_ref, gpm_ref,
                  o_ref, zs, cs, conv_scr):
    tm = z_ref.shape[0]
    t = pl.program_id(1)

    @pl.when(t == 0)
    def _():
        zs[0:POOL_HALO, :] = zm_ref[...]
        cs[0:CONV_HALO - N_META] = jnp.zeros((CONV_HALO - N_META, SUBLANES, LANES), F32)
        cs[CONV_HALO - N_META:CONV_HALO] = am_ref[...].reshape(N_META, SUBLANES, LANES)

    @pl.when(t > 0)
    def _():
        zs[0:POOL_HALO, :] = zs[tm:tm + POOL_HALO, :]
        cs[0:CONV_HALO] = cs[tm:tm + CONV_HALO]

    zs[POOL_HALO:POOL_HALO + tm, :] = z_ref[...]
    cs[CONV_HALO:CONV_HALO + tm] = a_ref[...].reshape(tm, SUBLANES, LANES)

    ya_parts = []
    for g, w in enumerate(POOL_WINDOWS):
        cols = slice(g * POOL_GROUP_DIM, (g + 1) * POOL_GROUP_DIM)
        zz = zs[:, cols]
        win = zz
        shift = 1
        while shift < w:
            win = win + pltpu.roll(win, shift, axis=0)
            shift *= 2
        d = (win[POOL_HALO:] * (1.0 / w) - zz[POOL_HALO:]).astype(BF16)
        y = jnp.dot(d, wgrp_ref[g], preferred_element_type=F32)
        ya_parts.append((y * pscale_ref[:, cols]).astype(BF16))
    pooled = jnp.concatenate(ya_parts, axis=-1)
    y_a = jnp.dot(pooled, wpo_ref[...], preferred_element_type=F32)

    first_tap = CONV_HALO - (CONV_KERNEL - 1)
    for r0 in range(0, tm, CONV_STEPS):
        acc = jnp.broadcast_to(bdw_ref[...][None], (CONV_STEPS, SUBLANES, LANES))
        for k in range(CONV_KERNEL):
            lo = r0 + first_tap + k
            acc = acc + cs[lo:lo + CONV_STEPS] * wdw_ref[k][None]
        conv_scr[r0:r0 + CONV_STEPS] = acc

    c = conv_scr[...].reshape(tm, CONV_WIDTH)
    mu = jnp.mean(c, axis=-1, keepdims=True)
    cc = c - mu
    var = jnp.mean(cc * cc, axis=-1, keepdims=True)
    ln = cc * lax.rsqrt(var + LN_EPS) * lng_ref[...] + lnb_ref[...]
    act = (ln * _sigmoid(ln)).astype(BF16)
    y_b = jnp.dot(act, wco_ref[...], preferred_element_type=F32)

    m = sga_ref[...].astype(F32) * y_a + sgb_ref[...].astype(F32) * y_b
    mo = jnp.dot(m.astype(BF16), wo_ref[...], preferred_element_type=F32)
    o_ref[...] = x_ref[...] + _rms_norm(mo, gpm_ref[...])


def _mixer(z, a, sga, sgb, x2d, z_meta, a_meta, w_grp, pool_scale, w_pool_out,
           w_dw, b_dw, ln_g, ln_b, w_conv_out, w_o, g_post, *, batch, seq):
    tm = MIX_TM
    nt = seq // tm
    row = lambda b, t: (b * nt + t, 0)
    const2 = lambda b, t: (0, 0)
    const3 = lambda b, t: (0, 0, 0)
    once = pl.Buffered(1)
    return pl.pallas_call(
        _mixer_kernel,
        out_shape=jax.ShapeDtypeStruct((batch * seq, D_MODEL), F32),
        grid=(batch, nt),
        in_specs=[
            pl.BlockSpec((tm, POOL_WIDTH), row),
            pl.BlockSpec((tm, CONV_WIDTH), row),
            pl.BlockSpec((tm, D_MODEL), row),
            pl.BlockSpec((tm, D_MODEL), row),
            pl.BlockSpec((tm, D_MODEL), row),
            pl.BlockSpec((N_META, POOL_WIDTH), const2, pipeline_mode=once),
            pl.BlockSpec((N_META, CONV_WIDTH), const2, pipeline_mode=once),
            pl.BlockSpec((len(POOL_WINDOWS), POOL_GROUP_DIM, POOL_GROUP_DIM), const3,
                         pipeline_mode=once),
            pl.BlockSpec((1, POOL_WIDTH), const2, pipeline_mode=once),
            pl.BlockSpec((POOL_WIDTH, D_MODEL), const2, pipeline_mode=once),
            pl.BlockSpec((CONV_KERNEL, SUBLANES, LANES), const3, pipeline_mode=once),
            pl.BlockSpec((SUBLANES, LANES), const2, pipeline_mode=once),
            pl.BlockSpec((1, CONV_WIDTH), const2, pipeline_mode=once),
            pl.BlockSpec((1, CONV_WIDTH), const2, pipeline_mode=once),
            pl.BlockSpec((CONV_WIDTH, D_MODEL), const2, pipeline_mode=once),
            pl.BlockSpec((D_MODEL, D_MODEL), const2, pipeline_mode=once),
            pl.BlockSpec((1, D_MODEL), const2, pipeline_mode=once),
        ],
        out_specs=pl.BlockSpec((tm, D_MODEL), row),
        scratch_shapes=[
            pltpu.VMEM((POOL_HALO + tm, POOL_WIDTH), F32),
            pltpu.VMEM((CONV_HALO + tm, SUBLANES, LANES), F32),
            pltpu.VMEM((tm, SUBLANES, LANES), F32),
        ],
        compiler_params=pltpu.CompilerParams(
            dimension_semantics=("arbitrary", "arbitrary"),
            vmem_limit_bytes=V7X_VMEM_LIMIT_BYTES),
        name="mixer",
    )(z, a, sga, sgb, x2d, z_meta, a_meta, w_grp, pool_scale, w_pool_out,
      w_dw, b_dw, ln_g, ln_b, w_conv_out, w_o, g_post)


def _mlp_step(h_ref, wup_ref, wdn_ref, o_ref, g1_ref, g2_ref, u_scr):
    k = pl.program_id(1)
    last = D_FF // MLP_TF - 1
    tm = h_ref.shape[0]
    mb = tm // MLP_ROW_BLOCKS
    row_blocks = [slice(r0, r0 + mb) for r0 in range(0, tm, mb)]

    def partial_out(rb):
        up = jnp.dot(u_scr[rb, :], wup_ref[...], preferred_element_type=F32)
        act = jnp.square(jnp.maximum(up, 0.0)).astype(BF16)
        return jnp.dot(act, wdn_ref[...], preferred_element_type=F32)

    def norm_in(rb):
        u_scr[rb, :] = _rms_norm(h_ref[rb, :], g1_ref[...]).astype(BF16)

    @pl.when(k == 0)
    def _():
        norm_in(row_blocks[0])
        for i, rb in enumerate(row_blocks):
            if i + 1 < len(row_blocks):
                norm_in(row_blocks[i + 1])
            o_ref[rb, :] = partial_out(rb)

    @pl.when(jnp.logical_and(k > 0, k < last))
    def _():
        o_ref[...] += partial_out(slice(None))

    @pl.when(k == last)
    def _():
        for rb in row_blocks:
            f = o_ref[rb, :] + partial_out(rb)
            o_ref[rb, :] = h_ref[rb, :] + _rms_norm(f, g2_ref[...])


def _mlp_kernel(h_hbm, g1_ref, wup_hbm, wdn_hbm, g2_ref, o_hbm, u_scr):
    m = h_hbm.shape[0]
    row = lambda i, k: (i, 0)
    pltpu.emit_pipeline(
        _mlp_step,
        grid=(m // MLP_TM, D_FF // MLP_TF),
        in_specs=[
            pl.BlockSpec((MLP_TM, D_MODEL), row,
                         pipeline_mode=pl.Buffered(2, use_lookahead=True)),
            pl.BlockSpec((D_MODEL, MLP_TF), lambda i, k: (0, k)),
            pl.BlockSpec((MLP_TF, D_MODEL), lambda i, k: (k, 0)),
        ],
        out_specs=[pl.BlockSpec((MLP_TM, D_MODEL), row)],
    )(h_hbm, wup_hbm, wdn_hbm, o_hbm, scratches=(g1_ref, g2_ref, u_scr))


def _mlp(h, g_pre, w_up, w_down, g_post):
    m = h.shape[0]
    hbm = pl.BlockSpec(memory_space=pl.ANY)
    vmem = pl.BlockSpec(memory_space=pltpu.VMEM)
    return pl.pallas_call(
        _mlp_kernel,
        out_shape=jax.ShapeDtypeStruct((m, D_MODEL), F32),
        in_specs=[hbm, vmem, hbm, hbm, vmem],
        out_specs=hbm,
        scratch_shapes=[pltpu.VMEM((MLP_TM, D_MODEL), BF16)],
        compiler_params=pltpu.CompilerParams(vmem_limit_bytes=V7X_VMEM_LIMIT_BYTES),
        name="mlp",
    )(h, g_pre, w_up, w_down, g_post)


def kernel(x, meta, g_pre_mix, w_in, w_pool_grp, pool_scale, w_pool_out, w_dw, b_dw, conv_ln_g, conv_ln_b, w_conv_out, w_o, g_post_mix, g_pre_mlp, w_up, w_down, g_post_mlp):
    batch, seq, _ = x.shape
    depth = w_in.shape[0]
    h = x.reshape(batch * seq, D_MODEL)
    meta_h = meta.astype(x.dtype)
    assert depth == 1, "meta rows are only carried as first-layer causal history"
    for l in range(depth):
        w_in_l = w_in[l]
        g_pre = g_pre_mix[l][None]
        z_meta, a_meta = _meta_proj(meta_h, g_pre, w_in_l)
        (z, a, sga, sgb), (w_pool_out_b, w_conv_out_b, w_o_b, w_up_b, w_down_b) = _in_proj(
            h, g_pre, w_in_l, [w_pool_out[l], w_conv_out[l], w_o[l], w_up[l], w_down[l]])
        h = _mixer(z, a, sga, sgb, h, z_meta, a_meta,
                   w_pool_grp[l].astype(BF16), pool_scale[l][None], w_pool_out_b,
                   w_dw[l].reshape(CONV_KERNEL, SUBLANES, LANES),
                   b_dw[l].reshape(SUBLANES, LANES), conv_ln_g[l][None], conv_ln_b[l][None],
                   w_conv_out_b, w_o_b, g_post_mix[l][None], batch=batch, seq=seq)
        h = _mlp(h, g_pre_mlp[l][None], w_up_b, w_down_b, g_post_mlp[l][None])
    return h.reshape(batch, seq, D_MODEL)
```
